```python
import functools
import jax, jax.numpy as jnp
from jax import lax
import numpy as np

D_MODEL = 1024
BATCH = 2
SEQ = 8192
DEPTH = 2
DEC_BATCH = 32
DEC_SEQ = 4
PAST_LEN = 16384
PAGE_SIZE = 128

N_HEADS = 16
HEAD_DIM = D_MODEL // N_HEADS
ATTN_DIM = N_HEADS * HEAD_DIM
CONV_DIM = D_MODEL
CONV_WIDTH = 31
MOBA_BLOCK = 256
MOBA_TOP_K = 3
Q_CHUNK = 128
D_FF = ((8 * D_MODEL // 3 + 127) // 128) * 128
N_IN = 2 * CONV_DIM + 3 * ATTN_DIM + 2 * D_MODEL
ROPE_THETA = 10000.0
RMS_EPS = 1e-6
LN_EPS = 1e-5
FFN_RES = 0.5
NEG_INF = -1e30

kernel_name = "macaron_conformer_moba_hybrid_step"


def rms_norm(x, g):
    xf = x.astype(jnp.float32)
    y = xf * lax.rsqrt(jnp.mean(xf * xf, axis=-1, keepdims=True) + RMS_EPS)
    return (y * g.astype(jnp.float32)).astype(x.dtype)


def layer_norm(x, g, b):
    xf = x.astype(jnp.float32)
    mu = jnp.mean(xf, axis=-1, keepdims=True)
    var = jnp.mean(jnp.square(xf - mu), axis=-1, keepdims=True)
    y = (xf - mu) * lax.rsqrt(var + LN_EPS) * g.astype(jnp.float32) + b.astype(jnp.float32)
    return y.astype(x.dtype)


def swiglu_ffn(x, w_gate, w_up, w_down):
    return (jax.nn.silu(x @ w_gate) * (x @ w_up)) @ w_down


def rope_tables(pos):
    inv = ROPE_THETA ** (-jnp.arange(0, HEAD_DIM, 2, dtype=jnp.float32) / HEAD_DIM)
    ang = pos.astype(jnp.float32)[:, None] * inv[None, :]
    return jnp.cos(ang), jnp.sin(ang)


def apply_rope(x, cos, sin):
    xf = x.astype(jnp.float32)
    half = HEAD_DIM // 2
    x1, x2 = xf[..., :half], xf[..., half:]
    c, s = cos[None, :, None, :], sin[None, :, None, :]
    return jnp.concatenate([x1 * c - x2 * s, x2 * c + x1 * s], axis=-1).astype(x.dtype)


def causal_depthwise_conv(u, buf, w, b):
    ext = jnp.concatenate([buf.astype(u.dtype), u], axis=1)
    y = lax.conv_general_dilated(ext, w[:, None, :].astype(u.dtype), window_strides=(1,), padding='VALID',
                                 dimension_numbers=('NWC', 'WIO', 'NWC'), feature_group_count=u.shape[-1])
    return y + b, ext[:, -(CONV_WIDTH - 1):]


def moba_prompt(q, k, v):
    bsz, s, h, hd = q.shape
    scale = hd ** -0.5
    nb = -(-s // MOBA_BLOCK)
    pad = nb * MOBA_BLOCK - s

    def to_blocks(t):
        t = jnp.pad(t, ((0, 0), (0, pad), (0, 0), (0, 0)))
        return t.reshape(bsz, nb, MOBA_BLOCK, h, hd).transpose(0, 3, 1, 2, 4)

    kb, vb = to_blocks(k), to_blocks(v)
    means = jnp.mean(kb.astype(jnp.float32), axis=3)
    n_chunks = s // Q_CHUNK
    qc = q.reshape(bsz, n_chunks, Q_CHUNK, h, hd).transpose(1, 0, 3, 2, 4)
    k_sel = min(MOBA_TOP_K, nb - 1)
    bi = jnp.arange(bsz)[:, None, None]
    hi = jnp.arange(h)[None, :, None]
    blk_ids = jnp.arange(nb)

    def chunk(args):
        qb, c = args
        q_pos = c * Q_CHUNK + jnp.arange(Q_CHUNK)
        blk = q_pos[0] // MOBA_BLOCK
        k_own = lax.dynamic_index_in_dim(kb, blk, axis=2, keepdims=False)
        v_own = lax.dynamic_index_in_dim(vb, blk, axis=2, keepdims=False)
        key_pos = blk * MOBA_BLOCK + jnp.arange(MOBA_BLOCK)
        l_own = jnp.einsum('bhqd,bhkd->bhqk', qb, k_own).astype(jnp.float32) * scale
        logits = [jnp.where(key_pos[None, :] <= q_pos[:, None], l_own, NEG_INF)]
        if k_sel > 0:
            gate = jnp.einsum('bhqd,bhnd->bhqn', qb.astype(jnp.float32), means)
            gate = jnp.where(blk_ids < blk, gate, -jnp.inf)
            _, top_i = lax.top_k(gate, k_sel)
            valid = top_i < blk
            for j in range(k_sel):
                k_j = kb[bi, hi, top_i[..., j]]
                l_j = jnp.einsum('bhqd,bhqkd->bhqk', qb, k_j).astype(jnp.float32) * scale
                logits.append(jnp.where(valid[..., j, None], l_j, NEG_INF))
        p = jax.nn.softmax(jnp.concatenate(logits, axis=-1), axis=-1).astype(v.dtype)
        out = jnp.einsum('bhqk,bhkd->bhqd', p[..., :MOBA_BLOCK], v_own)
        for j in range(k_sel):
            v_j = vb[bi, hi, top_i[..., j]]
            out = out + jnp.einsum('bhqk,bhqkd->bhqd', p[..., MOBA_BLOCK * (j + 1):MOBA_BLOCK * (j + 2)], v_j)
        return out

    out = lax.map(chunk, (qc, jnp.arange(n_chunks)))
    return out.transpose(1, 0, 3, 2, 4).reshape(bsz, s, h, hd)


def moba_sample(q, k_new, v_new, *, cache_k, cache_v, page_sums, page_table, layer):
    db, t, h, hd = q.shape
    scale = hd ** -0.5
    n_pages = page_table.shape[1]
    past = n_pages * PAGE_SIZE
    ppb = MOBA_BLOCK // PAGE_SIZE
    n_past_blk = past // MOBA_BLOCK
    own_first_page = n_past_blk * ppb
    n_own_pages = n_pages - own_first_page
    causal = jnp.tril(jnp.ones((t, t), dtype=bool))
    l_new = jnp.einsum('bqhd,bkhd->bqhk', q, k_new).astype(jnp.float32) * scale
    logits = [jnp.where(causal[None, :, None, :], l_new, NEG_INF)]
    if n_own_pages > 0:
        phys_own = page_table[:, own_first_page:]
        k_own = cache_k[layer, phys_own].transpose(0, 2, 1, 3, 4).reshape(db, h, n_own_pages * PAGE_SIZE, hd)
        v_own = cache_v[layer, phys_own].transpose(0, 2, 1, 3, 4).reshape(db, h, n_own_pages * PAGE_SIZE, hd)
        logits.append(jnp.einsum('bqhd,bhkd->bqhk', q, k_own).astype(jnp.float32) * scale)
    k_sel = min(MOBA_TOP_K, n_past_blk)
    if k_sel > 0:
        ps = page_sums[page_table[:, :own_first_page]]
        means = ps.reshape(db, n_past_blk, ppb, h, hd).sum(axis=2) / MOBA_BLOCK
        gate = jnp.einsum('bqhd,bnhd->bqhn', q.astype(jnp.float32), means)
        _, top_i = lax.top_k(gate, k_sel)
        logical = top_i[..., None] * ppb + jnp.arange(ppb)
        phys = page_table[jnp.arange(db)[:, None, None, None, None], logical]
        head_idx = jnp.arange(h)[None, None, :, None, None]
        k_s = cache_k[layer, phys, head_idx].reshape(db, t, h, k_sel * MOBA_BLOCK, hd)
        v_s = cache_v[layer, phys, head_idx].reshape(db, t, h, k_sel * MOBA_BLOCK, hd)
        logits.append(jnp.einsum('bqhd,bqhkd->bqhk', q, k_s).astype(jnp.float32) * scale)
    p = jax.nn.softmax(jnp.concatenate(logits, axis=-1), axis=-1).astype(v_new.dtype)
    out = jnp.einsum('bqhk,bkhd->bqhd', p[..., :t], v_new)
    off = t
    if n_own_pages > 0:
        n_own = n_own_pages * PAGE_SIZE
        out = out + jnp.einsum('bqhk,bhkd->bqhd', p[..., off:off + n_own], v_own)
        off += n_own
    if k_sel > 0:
        out = out + jnp.einsum('bqhk,bqhkd->bqhd', p[..., off:], v_s)
    return out


def token_mixer(h, conv_buf, cos, sin, attn_fn, w_in, conv_dw_w, conv_dw_b, conv_ln_g, conv_ln_b,
                w_conv_out, w_attn_out, w_out):
    bsz, s, _ = h.shape
    proj = h @ w_in
    splits = [CONV_DIM, 2 * CONV_DIM, 2 * CONV_DIM + ATTN_DIM, 2 * CONV_DIM + 2 * ATTN_DIM,
              2 * CONV_DIM + 3 * ATTN_DIM, 2 * CONV_DIM + 3 * ATTN_DIM + D_MODEL]
    u_a, u_b, q, k, v, g_conv, g_attn = jnp.split(proj, splits, axis=-1)
    glu = u_a * jax.nn.sigmoid(u_b)
    c, new_buf = causal_depthwise_conv(glu, conv_buf, conv_dw_w, conv_dw_b)
    conv_branch = jax.nn.silu(layer_norm(c, conv_ln_g, conv_ln_b)) @ w_conv_out
    q = apply_rope(q.reshape(bsz, s, N_HEADS, HEAD_DIM), cos, sin)
    k = apply_rope(k.reshape(bsz, s, N_HEADS, HEAD_DIM), cos, sin)
    v = v.reshape(bsz, s, N_HEADS, HEAD_DIM)
    attn_branch = attn_fn(q, k, v).reshape(bsz, s, ATTN_DIM) @ w_attn_out
    merged = jax.nn.sigmoid(g_conv) * conv_branch + jax.nn.sigmoid(g_attn) * attn_branch
    return merged @ w_out, k, v, new_buf


def setup_inputs(seed: int = 0) -> dict:
    key = jax.random.key(seed)
    ks = jax.random.split(key, 32)
    f32 = jnp.float32

    def w(k, shape, fan_in):
        return jax.random.normal(k, shape, f32) * (fan_in ** -0.5)

    def gain(k, shape):
        return 1.0 + 0.02 * jax.random.normal(k, shape, f32)

    n_pages = PAST_LEN // PAGE_SIZE
    n_used = DEC_BATCH * n_pages
    n_phys = n_used + max(1, n_used // 4)
    page_table = jax.random.permutation(ks[0], n_phys)[:n_used].reshape(DEC_BATCH, n_pages).astype(jnp.int32)
    return {
        "x_prompt": jax.random.normal(ks[1], (BATCH, SEQ, D_MODEL), f32),
        "x_sample": jax.random.normal(ks[2], (DEC_BATCH, DEC_SEQ, D_MODEL), f32),
        "cache_k": jax.random.normal(ks[3], (DEPTH, n_phys, N_HEADS, PAGE_SIZE, HEAD_DIM), f32),
        "cache_v": jax.random.normal(ks[4], (DEPTH, n_phys, N_HEADS, PAGE_SIZE, HEAD_DIM), f32),
        "cache_conv": 0.5 * jax.random.normal(ks[5], (DEPTH, DEC_BATCH, CONV_WIDTH - 1, CONV_DIM), f32),
        "page_table": page_table,
        "ffn1_norm": gain(ks[6], (DEPTH, D_MODEL)),
        "ffn1_w_gate": w(ks[7], (DEPTH, D_MODEL, D_FF), D_MODEL),
        "ffn1_w_up": w(ks[8], (DEPTH, D_MODEL, D_FF), D_MODEL),
        "ffn1_w_down": w(ks[9], (DEPTH, D_FF, D_MODEL), D_FF),
        "mix_norm": gain(ks[10], (DEPTH, D_MODEL)),
        "w_in": w(ks[11], (DEPTH, D_MODEL, N_IN), D_MODEL),
        "conv_dw_w": w(ks[12], (DEPTH, CONV_WIDTH, CONV_DIM), CONV_WIDTH),
        "conv_dw_b": 0.02 * jax.random.normal(ks[13], (DEPTH, CONV_DIM), f32),
        "conv_ln_g": gain(ks[14], (DEPTH, CONV_DIM)),
        "conv_ln_b": 0.02 * jax.random.normal(ks[15], (DEPTH, CONV_DIM), f32),
        "w_conv_out": w(ks[16], (DEPTH, CONV_DIM, D_MODEL), CONV_DIM),
        "w_attn_out": w(ks[17], (DEPTH, ATTN_DIM, D_MODEL), ATTN_DIM),
        "w_out": w(ks[18], (DEPTH, D_MODEL, D_MODEL), D_MODEL),
        "ffn2_norm": gain(ks[19], (DEPTH, D_MODEL)),
        "ffn2_w_gate": w(ks[20], (DEPTH, D_MODEL, D_FF), D_MODEL),
        "ffn2_w_up": w(ks[21], (DEPTH, D_MODEL, D_FF), D_MODEL),
        "ffn2_w_down": w(ks[22], (DEPTH, D_FF, D_MODEL), D_FF),
        "final_norm": gain(ks[23], (D_MODEL,)),
    }


def reference(x_prompt, x_sample, cache_k, cache_v, cache_conv, page_table,
              ffn1_norm, ffn1_w_gate, ffn1_w_up, ffn1_w_down,
              mix_norm, w_in, conv_dw_w, conv_dw_b, conv_ln_g, conv_ln_b,
              w_conv_out, w_attn_out, w_out,
              ffn2_norm, ffn2_w_gate, ffn2_w_up, ffn2_w_down, final_norm):

    def trunk(x, cos, sin, conv_bufs, attn_fns):
        k_rows, v_rows, bufs = [], [], []
        for l in range(DEPTH):
            x = x + FFN_RES * swiglu_ffn(rms_norm(x, ffn1_norm[l]), ffn1_w_gate[l], ffn1_w_up[l], ffn1_w_down[l])
            m, k, v, nbuf = token_mixer(rms_norm(x, mix_norm[l]), conv_bufs[l], cos, sin, attn_fns[l],
                                        w_in[l], conv_dw_w[l], conv_dw_b[l], conv_ln_g[l], conv_ln_b[l],
                                        w_conv_out[l], w_attn_out[l], w_out[l])
            x = x + m
            x = x + FFN_RES * swiglu_ffn(rms_norm(x, ffn2_norm[l]), ffn2_w_gate[l], ffn2_w_up[l], ffn2_w_down[l])
            k_rows.append(k)
            v_rows.append(v)
            bufs.append(nbuf)
        return rms_norm(x, final_norm), k_rows, v_rows, bufs

    bsz, s, _ = x_prompt.shape
    cos_p, sin_p = rope_tables(jnp.arange(s))
    zero_buf = jnp.zeros((bsz, CONV_WIDTH - 1, CONV_DIM), x_prompt.dtype)
    y_prompt, kp, vp, bp = trunk(x_prompt, cos_p, sin_p, [zero_buf] * DEPTH, [moba_prompt] * DEPTH)

    db, t, _ = x_sample.shape
    past = page_table.shape[1] * PAGE_SIZE
    cos_s, sin_s = rope_tables(past + jnp.arange(t))
    k_page_sums = jnp.sum(cache_k, axis=3, dtype=jnp.float32)
    attn_s = [functools.partial(moba_sample, cache_k=cache_k, cache_v=cache_v, page_sums=k_page_sums[l],
                                page_table=page_table, layer=l) for l in range(DEPTH)]
    y_sample, ksm, vsm, bsm = trunk(x_sample, cos_s, sin_s, [cache_conv[l] for l in range(DEPTH)], attn_s)

    def as_pages(r):
        return r.reshape(bsz, s // PAGE_SIZE, PAGE_SIZE, N_HEADS, HEAD_DIM).transpose(0, 1, 3, 2, 4)

    new_k_prompt = jnp.stack([as_pages(r) for r in kp])
    new_v_prompt = jnp.stack([as_pages(r) for r in vp])
    new_conv_prompt = jnp.stack(bp)
    new_k_sample = jnp.stack([r.transpose(0, 2, 1, 3) for r in ksm])
    new_v_sample = jnp.stack([r.transpose(0, 2, 1, 3) for r in vsm])
    new_conv_sample = jnp.stack(bsm)
    return (y_prompt, y_sample, new_k_prompt, new_v_prompt, new_conv_prompt, new_k_sample, new_v_sample, new_conv_sample)
```

```python
import functools

import jax
import jax.numpy as jnp
from jax import lax
from jax.experimental import pallas as pl
from jax.experimental.pallas import tpu as pltpu

F32 = jnp.float32
BF16 = jnp.bfloat16

N_HEADS = 16
HEAD_DIM = 64
CONV_WIDTH = 31
MOBA_BLOCK = 256
MOBA_TOP_K = 3
PAGE_SIZE = 128
ROPE_THETA = 10000.0
RMS_EPS = 1e-6
LN_EPS = 1e-5
FFN_RES = 0.5
NEG_INF = -1e30

LANES = 128
SUBLANES = 8
VMEM_LIMIT = 56 * 1024 * 1024

HEADS_PER_LANE_GROUP = LANES // HEAD_DIM
PAGES_PER_BLOCK = MOBA_BLOCK // PAGE_SIZE
HALO = 32
HALO_PAD = HALO - (CONV_WIDTH - 1)
MASK_NEG = -(2.0 ** 100)


def _resident(shape):
    nd = len(shape)
    return pl.BlockSpec(shape, lambda *_: (0,) * nd, pipeline_mode=pl.Buffered(1))


def _params(n_grid):
    return pltpu.CompilerParams(dimension_semantics=("arbitrary",) * n_grid, vmem_limit_bytes=VMEM_LIMIT)


def _rms(x, g):
    return x * lax.rsqrt(jnp.mean(x * x, axis=-1, keepdims=True) + RMS_EPS) * g


def _ffn_kernel(x_ref, g_ref, wg_ref, wu_ref, wd_ref, *rest, ff_chunk, final):
    if final:
        fn_ref, o_ref = rest
    else:
        (o_ref,) = rest
    x = x_ref[...]
    h = _rms(x, g_ref[...]).astype(BF16)
    d_ff = wg_ref.shape[1]
    y = None
    for c0 in range(0, d_ff, ff_chunk):
        gate = jnp.dot(h, wg_ref[:, c0:c0 + ff_chunk], preferred_element_type=F32)
        up = jnp.dot(h, wu_ref[:, c0:c0 + ff_chunk], preferred_element_type=F32)
        a = (gate * jax.nn.sigmoid(gate) * up).astype(BF16)
        part = jnp.dot(a, wd_ref[c0:c0 + ff_chunk, :], preferred_element_type=F32)
        y = part if y is None else y + part
    out = x + FFN_RES * y
    if final:
        out = _rms(out, fn_ref[...])
    o_ref[...] = out


def _ffn(x, norm_g, wg, wu, wd, final_g, tm):
    n, d = x.shape
    d_ff = wg.shape[1]
    ff_chunk = d_ff // 2 if (d_ff // 2) % LANES == 0 else d_ff
    final = final_g is not None
    tile = pl.BlockSpec((tm, d), lambda i: (i, 0))
    in_specs = [tile, _resident((1, d)), _resident((d, d_ff)), _resident((d, d_ff)), _resident((d_ff, d))]
    args = [x, norm_g, wg, wu, wd]
    if final:
        in_specs.append(_resident((1, d)))
        args.append(final_g)
    return pl.pallas_call(
        functools.partial(_ffn_kernel, ff_chunk=ff_chunk, final=final),
        grid=(n // tm,),
        in_specs=in_specs,
        out_specs=tile,
        out_shape=jax.ShapeDtypeStruct((n, d), F32),
        compiler_params=_params(1),
        name="ffn",
    )(*args)


def _rope(x, cos, sin_signed):
    tm, d = x.shape
    lane = lax.broadcasted_iota(jnp.int32, (tm, LANES), 1)
    first_half = (lane % HEAD_DIM) < (HEAD_DIM // 2)
    outs = []
    for g in range(d // LANES):
        xs = x[:, g * LANES:(g + 1) * LANES]
        partner = jnp.where(first_half,
                            pltpu.roll(xs, LANES - HEAD_DIM // 2, 1),
                            pltpu.roll(xs, HEAD_DIM // 2, 1))
        outs.append(xs * cos + partner * sin_signed)
    return jnp.concatenate(outs, axis=1)


def _mix_in_kernel(x_ref, g_ref, w_ref, cos_ref, sin_ref, glu_ref, gc_ref, ga_ref, q_ref, *rest, paged):
    d = x_ref.shape[1]
    tm = x_ref.shape[0]
    h = _rms(x_ref[...], g_ref[...]).astype(BF16)

    def proj(idx):
        return jnp.dot(h, w_ref[:, idx * d:(idx + 1) * d], preferred_element_type=F32)

    glu_ref[...] = proj(0) * jax.nn.sigmoid(proj(1))
    cos = cos_ref[...]
    sin_signed = sin_ref[...]
    q_ref[...] = _rope(proj(2), cos, sin_signed)
    k = _rope(proj(3), cos, sin_signed)
    v = proj(4)
    gc_ref[...] = proj(5)
    ga_ref[...] = proj(6)
    if paged:
        kb_ref, vb_ref, kp_ref, vp_ref, km_ref = rest
        kb_ref[...] = k.astype(BF16)
        vb_ref[...] = v.astype(BF16)
        for p in range(tm // PAGE_SIZE):
            rows = slice(p * PAGE_SIZE, (p + 1) * PAGE_SIZE)
            for hd in range(N_HEADS):
                cols = slice(hd * HEAD_DIM, (hd + 1) * HEAD_DIM)
                kp_ref[p, hd] = k[rows, cols]
                vp_ref[p, hd] = v[rows, cols]
        means = [jnp.mean(k[b * MOBA_BLOCK:(b + 1) * MOBA_BLOCK], axis=0, keepdims=True)
                 for b in range(tm // MOBA_BLOCK)]
        km_ref[0] = jnp.concatenate(means, axis=0)
    else:
        kf_ref, vf_ref = rest
        kf_ref[...] = k
        vf_ref[...] = v


def _mix_in(x, norm_g, w_in, cos, sin_signed, tm, paged):
    n, d = x.shape
    n_pos_tiles = cos.shape[0] // tm
    tile = pl.BlockSpec((tm, d), lambda i: (i, 0))
    table = pl.BlockSpec((tm, LANES), lambda i: (i % n_pos_tiles, 0))
    in_specs = [tile, _resident((1, d)), _resident(w_in.shape), table, table]
    out_specs = [tile] * 4
    out_shape = [jax.ShapeDtypeStruct((n, d), F32)] * 4
    if paged:
        pages = tm // PAGE_SIZE
        blocks = tm // MOBA_BLOCK
        page_spec = pl.BlockSpec((pages, N_HEADS, PAGE_SIZE, HEAD_DIM), lambda i: (i, 0, 0, 0))
        page_shape = jax.ShapeDtypeStruct((n // PAGE_SIZE, N_HEADS, PAGE_SIZE, HEAD_DIM), F32)
        out_specs += [tile, tile, page_spec, page_spec, pl.BlockSpec((1, blocks, d), lambda i: (i, 0, 0))]
        out_shape += [jax.ShapeDtypeStruct((n, d), BF16)] * 2 + [page_shape] * 2
        out_shape += [jax.ShapeDtypeStruct((n // tm, blocks, d), F32)]
    else:
        out_specs += [tile, tile]
        out_shape += [jax.ShapeDtypeStruct((n, d), F32)] * 2
    return pl.pallas_call(
        functools.partial(_mix_in_kernel, paged=paged),
        grid=(n // tm,),
        in_specs=in_specs,
        out_specs=out_specs,
        out_shape=out_shape,
        compiler_params=_params(1),
        name="mix_in_prompt" if paged else "mix_in_sample",
    )(x, norm_g, w_in, cos, sin_signed)


def _selection_bias(gate, valid, own, lane_f):
    g = jnp.where(valid, gate, -jnp.inf)
    bias = jnp.where(own, 0.0, MASK_NEG)
    for _ in range(MOBA_TOP_K):
        mx = jnp.max(g, axis=1, keepdims=True)
        first = jnp.min(jnp.where(g == mx, lane_f, float(LANES)), axis=1, keepdims=True)
        pick = lane_f == first
        bias = jnp.where(pick, jnp.where(valid, 0.0, bias), bias)
        g = jnp.where(pick, -jnp.inf, g)
    return bias


def _moba_prompt_kernel(q_ref, k_ref, v_ref, mean_ref, o_ref, qext_sc, m_sc, l_sc, acc_sc, *, tq):
    qt = pl.program_id(2)
    blocks_per_tile = tq // MOBA_BLOCK
    q = q_ref[0]
    lane = lax.broadcasted_iota(jnp.int32, (tq, LANES), 1)
    row = lax.broadcasted_iota(jnp.int32, (tq, LANES), 0)
    q_blk = qt * blocks_per_tile + row // MOBA_BLOCK
    lane_f = lane.astype(F32)
    means_t = mean_ref[0]
    scale = HEAD_DIM ** -0.5

    for hd in range(HEADS_PER_LANE_GROUP):
        in_head = (lane >= hd * HEAD_DIM) & (lane < (hd + 1) * HEAD_DIM)
        qm = jnp.where(in_head, q, 0.0)
        gate = jnp.dot(qm, means_t, precision=lax.Precision.HIGHEST, preferred_element_type=F32)
        bias = _selection_bias(gate, lane < q_blk, lane == q_blk, lane_f)
        qext_sc[hd] = jnp.concatenate([(qm * scale).astype(BF16), bias.astype(BF16)], axis=1)
        m_sc[hd] = jnp.full((tq, LANES), -jnp.inf, F32)
        l_sc[hd] = jnp.zeros((tq, LANES), F32)
        acc_sc[hd] = jnp.zeros((tq, LANES), F32)

    def visit(j, row0, causal):
        start = pl.multiple_of(j * MOBA_BLOCK, MOBA_BLOCK)
        kj = k_ref[0, pl.ds(start, MOBA_BLOCK), :]
        vj = v_ref[0, pl.ds(start, MOBA_BLOCK), :]
        blk_lane = lax.broadcasted_iota(jnp.int32, (MOBA_BLOCK, LANES), 1)
        block_id = jnp.where(blk_lane == j, 1.0, 0.0).astype(BF16)
        k_ext = jnp.concatenate([kj, block_id], axis=1)
        rows = slice(row0, tq)
        n_rows = tq - row0
        for hd in range(HEADS_PER_LANE_GROUP):
            s = lax.dot_general(qext_sc[hd, rows, :], k_ext, (((1,), (1,)), ((), ())),
                                preferred_element_type=F32)
            if causal:
                r_i = lax.broadcasted_iota(jnp.int32, (n_rows, MOBA_BLOCK), 0)
                c_i = lax.broadcasted_iota(jnp.int32, (n_rows, MOBA_BLOCK), 1)
                s = jnp.where(c_i <= r_i, s, MASK_NEG)
            m_prev = m_sc[hd, rows, :]
            m_next = jnp.maximum(m_prev, jnp.max(s, axis=1, keepdims=True))
            alpha = jnp.exp(m_prev - m_next)
            p = jnp.exp(s - jnp.concatenate([m_next] * (MOBA_BLOCK // LANES), axis=1))
            l_sc[hd, rows, :] = alpha * l_sc[hd, rows, :] + jnp.sum(p, axis=1, keepdims=True)
            pv = jnp.dot(p.astype(BF16), vj, preferred_element_type=F32)
            acc_sc[hd, rows, :] = alpha * acc_sc[hd, rows, :] + pv
            m_sc[hd, rows, :] = m_next

    for r in range(blocks_per_tile):
        visit(qt * blocks_per_tile + r, r * MOBA_BLOCK, True)

    def past(j, carry):
        visit(j, 0, False)
        return carry

    lax.fori_loop(0, qt * blocks_per_tile, past, 0)

    out = None
    for hd in range(HEADS_PER_LANE_GROUP):
        o_h = acc_sc[hd] / l_sc[hd]
        out = o_h if out is None else jnp.where(lane < hd * HEAD_DIM, out, o_h)
    o_ref[0] = out.astype(o_ref.dtype)


def _moba_prompt(q, kb, vb, means, tq):
    bsz, s, d = q.shape
    groups = d // LANES
    q_spec = pl.BlockSpec((1, tq, LANES), lambda b, g, i: (b, i, g))
    kv_spec = pl.BlockSpec((1, s, LANES), lambda b, g, i: (b, 0, g))
    mean_spec = pl.BlockSpec((1, LANES, LANES), lambda b, g, i: (b, g, 0))
    return pl.pallas_call(
        functools.partial(_moba_prompt_kernel, tq=tq),
        grid=(bsz, groups, s // tq),
        in_specs=[q_spec, kv_spec, kv_spec, mean_spec],
        out_specs=q_spec,
        out_shape=jax.ShapeDtypeStruct((bsz, s, d), BF16),
        scratch_shapes=[
            pltpu.VMEM((HEADS_PER_LANE_GROUP, tq, 2 * LANES), BF16),
            pltpu.VMEM((HEADS_PER_LANE_GROUP, tq, LANES), F32),
            pltpu.VMEM((HEADS_PER_LANE_GROUP, tq, LANES), F32),
            pltpu.VMEM((HEADS_PER_LANE_GROUP, tq, LANES), F32),
        ],
        compiler_params=_params(3),
        name="moba_prompt",
    )(q, kb, vb, means)


def _mix_out_tail(x, c, gc, ga, attn, lng_ref, lnb_ref, wco_ref, wao_ref, wo_ref):
    mu = jnp.mean(c, axis=-1, keepdims=True)
    cc = c - mu
    var = jnp.mean(cc * cc, axis=-1, keepdims=True)
    y = cc * lax.rsqrt(var + LN_EPS) * lng_ref[...] + lnb_ref[...]
    act = (y * jax.nn.sigmoid(y)).astype(BF16)
    conv_branch = jnp.dot(act, wco_ref[...], preferred_element_type=F32)
    attn_branch = jnp.dot(attn.astype(BF16), wao_ref[...], preferred_element_type=F32)
    merged = jax.nn.sigmoid(gc) * conv_branch + jax.nn.sigmoid(ga) * attn_branch
    return x + jnp.dot(merged.astype(BF16), wo_ref[...], preferred_element_type=F32)


def _mix_out_seq_kernel(x_ref, glu_ref, halo_ref, hist_ref, gc_ref, ga_ref, attn_ref, cw_ref, cb_ref,
                        lng_ref, lnb_ref, wco_ref, wao_ref, wo_ref, o_ref, ext_sc, c_sc,
                        *, tiles_per_seq, row_chunk, col_chunk):
    tm, d = x_ref.shape
    first = (pl.program_id(0) % tiles_per_seq) == 0
    ext_sc[0:HALO, :] = jnp.where(first, hist_ref[0], halo_ref[...])
    ext_sc[HALO:HALO + tm, :] = glu_ref[...]
    for r0 in range(0, tm, row_chunk):
        for c0 in range(0, d, col_chunk):
            cols = slice(c0, c0 + col_chunk)
            acc = jnp.broadcast_to(cb_ref[:, cols], (row_chunk, col_chunk))
            for j in range(CONV_WIDTH):
                start = r0 + HALO_PAD + j
                acc = acc + cw_ref[j:j + 1, cols] * ext_sc[start:start + row_chunk, cols]
            c_sc[r0:r0 + row_chunk, cols] = acc
    o_ref[...] = _mix_out_tail(x_ref[...], c_sc[...], gc_ref[...], ga_ref[...], attn_ref[...],
                               lng_ref, lnb_ref, wco_ref, wao_ref, wo_ref)


def _mix_out_seq(x, glu, hist, gc, ga, attn, cw, cb, lng, lnb, wco, wao, wo, tm, seq_len):
    n, d = x.shape
    tiles_per_seq = seq_len // tm
    halo_per_tile = tm // HALO
    tile = pl.BlockSpec((tm, d), lambda i: (i, 0))
    halo = pl.BlockSpec((HALO, d), lambda i: (jnp.maximum(i * halo_per_tile - 1, 0), 0))
    hist_spec = pl.BlockSpec((1, HALO, d), lambda i: (i // tiles_per_seq, 0, 0))
    in_specs = [tile, tile, halo, hist_spec, tile, tile, tile,
                _resident(cw.shape), _resident((1, d)), _resident((1, d)), _resident((1, d)),
                _resident((d, d)), _resident((d, d)), _resident((d, d))]
    return pl.pallas_call(
        functools.partial(_mix_out_seq_kernel, tiles_per_seq=tiles_per_seq, row_chunk=32, col_chunk=512),
        grid=(n // tm,),
        in_specs=in_specs,
        out_specs=tile,
        out_shape=jax.ShapeDtypeStruct((n, d), F32),
        scratch_shapes=[pltpu.VMEM((HALO + tm, d), F32), pltpu.VMEM((tm, d), F32)],
        compiler_params=_params(1),
        name="mix_out_prompt",
    )(x, glu, glu, hist, gc, ga, attn, cw, cb, lng, lnb, wco, wao, wo)


def _mix_out_steps_kernel(x_ref, ext_ref, gc_ref, ga_ref, attn_ref, cw_ref, cb_ref,
                          lng_ref, lnb_ref, wco_ref, wao_ref, wo_ref, o_ref, c_sc):
    n_ext, n_seq, d = ext_ref.shape
    for t in range(n_ext - (CONV_WIDTH - 1)):
        acc = jnp.broadcast_to(cb_ref[...], (n_seq, d))
        for j in range(CONV_WIDTH):
            acc = acc + cw_ref[j:j + 1, :] * ext_ref[t + j]
        c_sc[t * n_seq:(t + 1) * n_seq, :] = acc
    o_ref[...] = _mix_out_tail(x_ref[...], c_sc[...], gc_ref[...], ga_ref[...], attn_ref[...],
                               lng_ref, lnb_ref, wco_ref, wao_ref, wo_ref)


def _mix_out_steps(x, ext, gc, ga, attn, cw, cb, lng, lnb, wco, wao, wo):
    n, d = x.shape
    full = pl.BlockSpec((n, d), lambda i: (0, 0))
    in_specs = [full, pl.BlockSpec(ext.shape, lambda i: (0, 0, 0)), full, full, full,
                _resident(cw.shape), _resident((1, d)), _resident((1, d)), _resident((1, d)),
                _resident((d, d)), _resident((d, d)), _resident((d, d))]
    return pl.pallas_call(
        _mix_out_steps_kernel,
        grid=(1,),
        in_specs=in_specs,
        out_specs=full,
        out_shape=jax.ShapeDtypeStruct((n, d), F32),
        scratch_shapes=[pltpu.VMEM((n, d), F32)],
        compiler_params=_params(1),
        name="mix_out_sample",
    )(x, ext, gc, ga, attn, cw, cb, lng, lnb, wco, wao, wo)


def _page_sum_kernel(k_ref, o_ref):
    o_ref[0] = jnp.sum(k_ref[0], axis=2)


def _page_sums(cache_k, pages_per_step):
    depth, n_phys, h, ps, hd = cache_k.shape
    return pl.pallas_call(
        _page_sum_kernel,
        grid=(depth, n_phys // pages_per_step),
        in_specs=[pl.BlockSpec((1, pages_per_step, h, ps, hd), lambda l, i: (l, i, 0, 0, 0))],
        out_specs=pl.BlockSpec((1, pages_per_step, h, hd), lambda l, i: (l, i, 0, 0)),
        out_shape=jax.ShapeDtypeStruct((depth, n_phys, h, hd), F32),
        compiler_params=_params(2),
        name="page_sums",
    )(cache_k)


def _sample_gate_kernel(q_ref, ps_ref, top_ref):
    n_blk = ps_ref.shape[1]
    d = q_ref.shape[2]
    block_sum = ps_ref[0, :, 0:d]
    for r in range(1, PAGES_PER_BLOCK):
        block_sum = block_sum + ps_ref[0, :, r * d:(r + 1) * d]
    means = block_sum / MOBA_BLOCK
    ch = lax.broadcasted_iota(jnp.int32, (d, LANES), 0)
    col = lax.broadcasted_iota(jnp.int32, (d, LANES), 1)
    head_of_channel = jnp.where(ch // HEAD_DIM == col, 1.0, 0.0)
    blk_f = lax.broadcasted_iota(jnp.int32, (n_blk, LANES), 0).astype(F32)
    rows = []
    for t in range(q_ref.shape[1]):
        prod = means * q_ref[0, t:t + 1, :]
        g = jnp.dot(prod, head_of_channel, precision=lax.Precision.HIGHEST, preferred_element_type=F32)
        for _ in range(MOBA_TOP_K):
            mx = jnp.max(g, axis=0, keepdims=True)
            first = jnp.min(jnp.where(g == mx, blk_f, float(n_blk)), axis=0, keepdims=True)
            rows.append(first)
            g = jnp.where(blk_f == first, -jnp.inf, g)
    top_ref[0] = jnp.concatenate(rows, axis=0).astype(jnp.int32)


def _sample_gate(q, ps):
    n_seq, t, d = q.shape
    return pl.pallas_call(
        _sample_gate_kernel,
        grid=(n_seq,),
        in_specs=[pl.BlockSpec((1, t, d), lambda b: (b, 0, 0)),
                  pl.BlockSpec((1,) + ps.shape[1:], lambda b: (b, 0, 0))],
        out_specs=pl.BlockSpec((1, t * MOBA_TOP_K, LANES), lambda b: (b, 0, 0)),
        out_shape=jax.ShapeDtypeStruct((n_seq, t * MOBA_TOP_K, LANES), jnp.int32),
        compiler_params=_params(1),
        name="sample_gate",
    )(q, ps)


def _sample_attn_kernel(top_ref, pt_ref, q_ref, kn_ref, vn_ref, ck_ref, cv_ref, o_ref,
                        kbuf, vbuf, sem, *, layer, n_tok):
    b = pl.program_id(0)
    n_comb = n_tok * N_HEADS
    scale = HEAD_DIM ** -0.5

    def copies(c, slot):
        hd = c % N_HEADS
        out = []
        for r in range(MOBA_TOP_K):
            blk = top_ref[(b * n_comb + c) * MOBA_TOP_K + r]
            for pg in range(PAGES_PER_BLOCK):
                phys = pt_ref[b, blk * PAGES_PER_BLOCK + pg]
                dst = pl.ds((r * PAGES_PER_BLOCK + pg) * PAGE_SIZE, PAGE_SIZE)
                out.append(pltpu.make_async_copy(ck_ref.at[layer, phys, hd], kbuf.at[slot, dst, :], sem.at[0, slot]))
                out.append(pltpu.make_async_copy(cv_ref.at[layer, phys, hd], vbuf.at[slot, dst, :], sem.at[1, slot]))
        return out

    for cp in copies(0, 0):
        cp.start()

    def body(c, carry):
        slot = c % 2

        @pl.when(c + 1 < n_comb)
        def _():
            for cp in copies(c + 1, 1 - slot):
                cp.start()

        for cp in copies(c, slot):
            cp.wait()
        hd = c % N_HEADS
        t = c // N_HEADS
        qc = q_ref[0, pl.ds(c, 1), :]
        kn = kn_ref[0, hd]
        vn = vn_ref[0, hd]
        s = jnp.sum(kbuf[slot] * qc, axis=1, keepdims=True) * scale
        s_new = jnp.sum(kn * qc, axis=1, keepdims=True) * scale
        tok = lax.broadcasted_iota(jnp.int32, s_new.shape, 0)
        s_new = jnp.where(tok <= t, s_new, NEG_INF)
        m = jnp.maximum(jnp.max(s, axis=0, keepdims=True), jnp.max(s_new, axis=0, keepdims=True))
        p = jnp.exp(s - m)
        p_new = jnp.exp(s_new - m)
        denom = jnp.sum(p, axis=0, keepdims=True) + jnp.sum(p_new, axis=0, keepdims=True)
        num = jnp.sum(p * vbuf[slot], axis=0, keepdims=True) + jnp.sum(p_new * vn, axis=0, keepdims=True)
        o_ref[0, pl.ds(c, 1), :] = num / denom
        return carry

    lax.fori_loop(0, n_comb, body, 0)


def _sample_attn(top_flat, page_table, q, kn, vn, cache_k, cache_v, layer):
    n_seq, n_comb, hd = q.shape
    n_tok = n_comb // N_HEADS
    rows = MOBA_TOP_K * MOBA_BLOCK
    grid_spec = pltpu.PrefetchScalarGridSpec(
        num_scalar_prefetch=2,
        grid=(n_seq,),
        in_specs=[pl.BlockSpec((1, n_comb, hd), lambda b, *_: (b, 0, 0)),
                  pl.BlockSpec((1, N_HEADS, n_tok, hd), lambda b, *_: (b, 0, 0, 0)),
                  pl.BlockSpec((1, N_HEADS, n_tok, hd), lambda b, *_: (b, 0, 0, 0)),
                  pl.BlockSpec(memory_space=pl.ANY),
                  pl.BlockSpec(memory_space=pl.ANY)],
        out_specs=pl.BlockSpec((1, n_comb, hd), lambda b, *_: (b, 0, 0)),
        scratch_shapes=[pltpu.VMEM((2, rows, hd), F32), pltpu.VMEM((2, rows, hd), F32),
                        pltpu.SemaphoreType.DMA((2, 2))],
    )
    return pl.pallas_call(
        functools.partial(_sample_attn_kernel, layer=layer, n_tok=n_tok),
        grid_spec=grid_spec,
        out_shape=jax.ShapeDtypeStruct((n_seq, n_comb, hd), F32),
        compiler_params=_params(1),
        name="sample_attn",
    )(top_flat, page_table, q, kn, vn, cache_k, cache_v)


def _rope_tables(pos):
    inv = ROPE_THETA ** (-jnp.arange(0, HEAD_DIM, 2, dtype=F32) / HEAD_DIM)
    ang = pos.astype(F32)[:, None] * inv[None, :]
    cos, sin = jnp.cos(ang), jnp.sin(ang)
    cos_g = jnp.concatenate([cos, cos] * HEADS_PER_LANE_GROUP, axis=1)
    sin_g = jnp.concatenate([-sin, sin] * HEADS_PER_LANE_GROUP, axis=1)
    return cos_g, sin_g


def kernel(x_prompt, x_sample, cache_k, cache_v, cache_conv, page_table,
           ffn1_norm, ffn1_w_gate, ffn1_w_up, ffn1_w_down,
           mix_norm, w_in, conv_dw_w, conv_dw_b, conv_ln_g, conv_ln_b,
           w_conv_out, w_attn_out, w_out,
           ffn2_norm, ffn2_w_gate, ffn2_w_up, ffn2_w_down, final_norm):
    bsz, seq, d = x_prompt.shape
    n_seq, n_tok, _ = x_sample.shape
    depth = w_in.shape[0]
    n_pages = page_table.shape[1]
    past = n_pages * PAGE_SIZE
    assert d == N_HEADS * HEAD_DIM and past % MOBA_BLOCK == 0 and n_pages // PAGES_PER_BLOCK >= MOBA_TOP_K
    assert seq % MOBA_BLOCK == 0 and seq >= CONV_WIDTH - 1 and seq // MOBA_BLOCK <= LANES

    tm_ffn, tm_mix, tq = 512, 256, 1024
    n_p, n_s = bsz * seq, n_seq * n_tok

    bf = lambda w: w.astype(BF16)
    row = lambda g: g.reshape(1, -1)

    cos_p, sin_p = _rope_tables(jnp.arange(seq))
    cos_s, sin_s = _rope_tables(past + jnp.repeat(jnp.arange(n_tok), n_seq))

    xp = x_prompt.reshape(n_p, d)
    xs = x_sample.transpose(1, 0, 2).reshape(n_s, d)
    sums = _page_sums(cache_k, pages_per_step=8)
    zero_hist = jnp.zeros((bsz, HALO, d), F32)

    kp_out, vp_out, convp_out, ks_out, vs_out, convs_out = [], [], [], [], [], []
    for l in range(depth):
        last = l == depth - 1
        f1 = (row(ffn1_norm[l]), bf(ffn1_w_gate[l]), bf(ffn1_w_up[l]), bf(ffn1_w_down[l]))
        f2 = (row(ffn2_norm[l]), bf(ffn2_w_gate[l]), bf(ffn2_w_up[l]), bf(ffn2_w_down[l]))
        fin = row(final_norm) if last else None
        w_in_l = bf(w_in[l])
        mix_w = (conv_dw_w[l], row(conv_dw_b[l]), row(conv_ln_g[l]), row(conv_ln_b[l]),
                 bf(w_conv_out[l]), bf(w_attn_out[l]), bf(w_out[l]))

        xp = _ffn(xp, *f1, None, tm_ffn)
        glu, gc, ga, q, kb, vb, kpg, vpg, kmean = _mix_in(xp, row(mix_norm[l]), w_in_l, cos_p, sin_p, tm_mix, True)
        n_blk = seq // MOBA_BLOCK
        means = jnp.pad(kmean.reshape(bsz, n_blk, d), ((0, 0), (0, LANES - n_blk), (0, 0))).transpose(0, 2, 1)
        attn = _moba_prompt(q.reshape(bsz, seq, d), kb.reshape(bsz, seq, d), vb.reshape(bsz, seq, d), means, tq)
        xp = _mix_out_seq(xp, glu, zero_hist, gc, ga, attn.reshape(n_p, d), *mix_w, tm_mix, seq)
        xp = _ffn(xp, *f2, fin, tm_ffn)
        kp_out.append(kpg.reshape(bsz, seq // PAGE_SIZE, N_HEADS, PAGE_SIZE, HEAD_DIM))
        vp_out.append(vpg.reshape(bsz, seq // PAGE_SIZE, N_HEADS, PAGE_SIZE, HEAD_DIM))
        convp_out.append(glu.reshape(bsz, seq, d)[:, seq - (CONV_WIDTH - 1):])

        xs = _ffn(xs, *f1, None, n_s)
        glu_s, gc_s, ga_s, q_s, k_s, v_s = _mix_in(xs, row(mix_norm[l]), w_in_l, cos_s, sin_s, n_s, False)
        to_seq_major = lambda a: a.reshape(n_tok, n_seq, d).transpose(1, 0, 2)
        k_new = to_seq_major(k_s).reshape(n_seq, n_tok, N_HEADS, HEAD_DIM).transpose(0, 2, 1, 3)
        v_new = to_seq_major(v_s).reshape(n_seq, n_tok, N_HEADS, HEAD_DIM).transpose(0, 2, 1, 3)
        q_b = to_seq_major(q_s)
        ps = sums[l][page_table].reshape(n_seq, n_pages // PAGES_PER_BLOCK, PAGES_PER_BLOCK * d)
        top = _sample_gate(q_b, ps)[:, :, :N_HEADS]
        top_flat = top.reshape(n_seq, n_tok, MOBA_TOP_K, N_HEADS).transpose(0, 1, 3, 2).reshape(-1)
        attn_s = _sample_attn(top_flat, page_table, q_b.reshape(n_seq, n_tok * N_HEADS, HEAD_DIM),
                              k_new, v_new, cache_k, cache_v, l)
        attn_s = attn_s.reshape(n_seq, n_tok, d).transpose(1, 0, 2).reshape(n_s, d)
        ext = jnp.concatenate([cache_conv[l].transpose(1, 0, 2), glu_s.reshape(n_tok, n_seq, d)], axis=0)
        xs = _mix_out_steps(xs, ext, gc_s, ga_s, attn_s, *mix_w)
        xs = _ffn(xs, *f2, fin, n_s)
        ks_out.append(k_new)
        vs_out.append(v_new)
        convs_out.append(jnp.concatenate([cache_conv[l][:, n_tok:], to_seq_major(glu_s)], axis=1))

    y_prompt = xp.reshape(bsz, seq, d)
    y_sample = xs.reshape(n_tok, n_seq, d).transpose(1, 0, 2)
    return (y_prompt, y_sample, jnp.stack(kp_out), jnp.stack(vp_out), jnp.stack(convp_out),
            jnp.stack(ks_out), jnp.stack(vs_out), jnp.stack(convs_out))
```

```python
import functools

import jax
import jax.numpy as jnp
from jax import lax
from jax.experimental import pallas as pl
from jax.experimental.pallas import tpu as pltpu

F32 = jnp.float32
BF16 = jnp.bfloat16

N_HEADS = 16
HEAD_DIM = 64
CONV_WIDTH = 31
MOBA_BLOCK = 256
MOBA_TOP_K = 3
PAGE_SIZE = 128
ROPE_THETA = 10000.0
RMS_EPS = 1e-6
LN_EPS = 1e-5
FFN_RES = 0.5
NEG_INF = -1e30

LANES = 128
SUBLANES = 8
VMEM_LIMIT = 56 * 1024 * 1024

HEADS_PER_LANE_GROUP = LANES // HEAD_DIM
PAGES_PER_BLOCK = MOBA_BLOCK // PAGE_SIZE
HALO = 32
HALO_PAD = HALO - (CONV_WIDTH - 1)
MASK_NEG = -(2.0 ** 100)


def _resident(shape):
    nd = len(shape)
    return pl.BlockSpec(shape, lambda *_: (0,) * nd, pipeline_mode=pl.Buffered(1))


def _params(n_grid):
    return pltpu.CompilerParams(dimension_semantics=("arbitrary",) * n_grid, vmem_limit_bytes=VMEM_LIMIT)


def _rms(x, g):
    return x * lax.rsqrt(jnp.mean(x * x, axis=-1, keepdims=True) + RMS_EPS) * g


def _ffn_kernel(x_ref, g_ref, wg_ref, wu_ref, wd_ref, *rest, ff_chunk, final):
    if final:
        fn_ref, o_ref = rest
    else:
        (o_ref,) = rest
    x = x_ref[...]
    h = _rms(x, g_ref[...]).astype(BF16)
    d_ff = wg_ref.shape[1]
    y = None
    for c0 in range(0, d_ff, ff_chunk):
        gate = jnp.dot(h, wg_ref[:, c0:c0 + ff_chunk], preferred_element_type=F32)
        up = jnp.dot(h, wu_ref[:, c0:c0 + ff_chunk], preferred_element_type=F32)
        a = (gate * jax.nn.sigmoid(gate) * up).astype(BF16)
        part = jnp.dot(a, wd_ref[c0:c0 + ff_chunk, :], preferred_element_type=F32)
        y = part if y is None else y + part
    out = x + FFN_RES * y
    if final:
        out = _rms(out, fn_ref[...])
    o_ref[...] = out


def _ffn(x, norm_g, wg, wu, wd, final_g, tm):
    n, d = x.shape
    d_ff = wg.shape[1]
    ff_chunk = d_ff // 2 if (d_ff // 2) % LANES == 0 else d_ff
    final = final_g is not None
    tile = pl.BlockSpec((tm, d), lambda i: (i, 0))
    in_specs = [tile, _resident((1, d)), _resident((d, d_ff)), _resident((d, d_ff)), _resident((d_ff, d))]
    args = [x, norm_g, wg, wu, wd]
    if final:
        in_specs.append(_resident((1, d)))
        args.append(final_g)
    return pl.pallas_call(
        functools.partial(_ffn_kernel, ff_chunk=ff_chunk, final=final),
        grid=(n // tm,),
        in_specs=in_specs,
        out_specs=tile,
        out_shape=jax.ShapeDtypeStruct((n, d), F32),
        compiler_params=_params(1),
        name="ffn",
    )(*args)


def _rope(x, cos, sin_signed):
    tm, d = x.shape
    lane = lax.broadcasted_iota(jnp.int32, (tm, LANES), 1)
    first_half = (lane % HEAD_DIM) < (HEAD_DIM // 2)
    outs = []
    for g in range(d // LANES):
        xs = x[:, g * LANES:(g + 1) * LANES]
        partner = jnp.where(first_half,
                            pltpu.roll(xs, LANES - HEAD_DIM // 2, 1),
                            pltpu.roll(xs, HEAD_DIM // 2, 1))
        outs.append(xs * cos + partner * sin_signed)
    return jnp.concatenate(outs, axis=1)


def _mix_in_kernel(x_ref, g_ref, w_ref, cos_ref, sin_ref, glu_ref, gc_ref, ga_ref, q_ref, *rest, paged):
    d = x_ref.shape[1]
    tm = x_ref.shape[0]
    h = _rms(x_ref[...], g_ref[...]).astype(BF16)

    def proj(idx):
        return jnp.dot(h, w_ref[:, idx * d:(idx + 1) * d], preferred_element_type=F32)

    glu_ref[...] = proj(0) * jax.nn.sigmoid(proj(1))
    cos = cos_ref[...]
    sin_signed = sin_ref[...]
    q_ref[...] = _rope(proj(2), cos, sin_signed)
    k = _rope(proj(3), cos, sin_signed)
    v = proj(4)
    gc_ref[...] = proj(5)
    ga_ref[...] = proj(6)
    if paged:
        kb_ref, vb_ref, kp_ref, vp_ref, km_ref = rest
        kb_ref[...] = k.astype(BF16)
        vb_ref[...] = v.astype(BF16)
        k_t = k.T
        v_t = v.T
        for p in range(tm // PAGE_SIZE):
            toks = slice(p * PAGE_SIZE, (p + 1) * PAGE_SIZE)
            for hd in range(N_HEADS):
                dims = slice(hd * HEAD_DIM, (hd + 1) * HEAD_DIM)
                kp_ref[p, hd] = k_t[dims, toks]
                vp_ref[p, hd] = v_t[dims, toks]
        means = [jnp.mean(k[b * MOBA_BLOCK:(b + 1) * MOBA_BLOCK], axis=0, keepdims=True)
                 for b in range(tm // MOBA_BLOCK)]
        km_ref[0] = jnp.concatenate(means, axis=0)
    else:
        kf_ref, vf_ref = rest
        kf_ref[...] = k
        vf_ref[...] = v


def _mix_in(x, norm_g, w_in, cos, sin_signed, tm, paged):
    n, d = x.shape
    n_pos_tiles = cos.shape[0] // tm
    tile = pl.BlockSpec((tm, d), lambda i: (i, 0))
    table = pl.BlockSpec((tm, LANES), lambda i: (i % n_pos_tiles, 0))
    in_specs = [tile, _resident((1, d)), _resident(w_in.shape), table, table]
    out_specs = [tile] * 4
    out_shape = [jax.ShapeDtypeStruct((n, d), F32)] * 4
    if paged:
        pages = tm // PAGE_SIZE
        blocks = tm // MOBA_BLOCK
        page_spec = pl.BlockSpec((pages, N_HEADS, HEAD_DIM, PAGE_SIZE), lambda i: (i, 0, 0, 0))
        page_shape = jax.ShapeDtypeStruct((n // PAGE_SIZE, N_HEADS, HEAD_DIM, PAGE_SIZE), F32)
        out_specs += [tile, tile, page_spec, page_spec, pl.BlockSpec((1, blocks, d), lambda i: (i, 0, 0))]
        out_shape += [jax.ShapeDtypeStruct((n, d), BF16)] * 2 + [page_shape] * 2
        out_shape += [jax.ShapeDtypeStruct((n // tm, blocks, d), F32)]
    else:
        out_specs += [tile, tile]
        out_shape += [jax.ShapeDtypeStruct((n, d), F32)] * 2
    return pl.pallas_call(
        functools.partial(_mix_in_kernel, paged=paged),
        grid=(n // tm,),
        in_specs=in_specs,
        out_specs=out_specs,
        out_shape=out_shape,
        compiler_params=_params(1),
        name="mix_in_prompt" if paged else "mix_in_sample",
    )(x, norm_g, w_in, cos, sin_signed)


def _selection_bias(gate, valid, own, lane_f):
    g = jnp.where(valid, gate, -jnp.inf)
    bias = jnp.where(own, 0.0, MASK_NEG)
    for _ in range(MOBA_TOP_K):
        mx = jnp.max(g, axis=1, keepdims=True)
        first = jnp.min(jnp.where(g == mx, lane_f, float(LANES)), axis=1, keepdims=True)
        pick = lane_f == first
        bias = jnp.where(pick, jnp.where(valid, 0.0, bias), bias)
        g = jnp.where(pick, -jnp.inf, g)
    return bias


def _moba_prompt_kernel(q_ref, k_ref, v_ref, mean_ref, o_ref, qext_sc, m_sc, acc_sc, *, tq, span):
    qt = pl.program_id(2)
    blocks_per_tile = tq // MOBA_BLOCK
    q = q_ref[0]
    lane = lax.broadcasted_iota(jnp.int32, (tq, LANES), 1)
    row = lax.broadcasted_iota(jnp.int32, (tq, LANES), 0)
    q_blk = qt * blocks_per_tile + row // MOBA_BLOCK
    lane_f = lane.astype(F32)
    means_t = mean_ref[0]
    scale = HEAD_DIM ** -0.5

    for hd in range(HEADS_PER_LANE_GROUP):
        in_head = (lane >= hd * HEAD_DIM) & (lane < (hd + 1) * HEAD_DIM)
        qm = jnp.where(in_head, q, 0.0)
        gate = jnp.dot(qm, means_t, precision=lax.Precision.HIGHEST, preferred_element_type=F32)
        bias = _selection_bias(gate, lane < q_blk, lane == q_blk, lane_f)
        qext_sc[hd] = jnp.concatenate([(qm * scale).astype(BF16), bias.astype(BF16)], axis=1)
        m_sc[hd] = jnp.full((tq, LANES), -jnp.inf, F32)
        acc_sc[hd] = jnp.zeros((tq, 2 * LANES), F32)

    def visit(j0, n_blk, row0, causal):
        n_keys = n_blk * MOBA_BLOCK
        start = pl.multiple_of(j0 * MOBA_BLOCK, MOBA_BLOCK)
        kj = k_ref[0, pl.ds(start, n_keys), :]
        vj = v_ref[0, pl.ds(start, n_keys), :]
        key_lane = lax.broadcasted_iota(jnp.int32, (n_keys, LANES), 1)
        key_blk = j0 + lax.broadcasted_iota(jnp.int32, (n_keys, LANES), 0) // MOBA_BLOCK
        k_ext = jnp.concatenate([kj, jnp.where(key_lane == key_blk, 1.0, 0.0).astype(BF16)], axis=1)
        v_ext = jnp.concatenate([vj, jnp.ones((n_keys, LANES), BF16)], axis=1)
        rows = slice(row0, tq)
        n_rows = tq - row0
        for hd in range(HEADS_PER_LANE_GROUP):
            s = lax.dot_general(qext_sc[hd, rows, :], k_ext, (((1,), (1,)), ((), ())),
                                preferred_element_type=F32)
            if causal:
                r_i = lax.broadcasted_iota(jnp.int32, (n_rows, n_keys), 0)
                c_i = lax.broadcasted_iota(jnp.int32, (n_rows, n_keys), 1)
                s = jnp.where(c_i <= r_i, s, MASK_NEG)
            m_prev = m_sc[hd, rows, :]
            m_next = jnp.maximum(m_prev, jnp.max(s, axis=1, keepdims=True))
            alpha = jnp.exp(m_prev - m_next)
            p = jnp.exp((s - jnp.concatenate([m_next] * (n_keys // LANES), axis=1)).astype(BF16))
            pv = jnp.dot(p, v_ext, preferred_element_type=F32)
            acc_sc[hd, rows, :] = jnp.concatenate([alpha, alpha], axis=1) * acc_sc[hd, rows, :] + pv
            m_sc[hd, rows, :] = m_next

    for r in range(blocks_per_tile):
        visit(qt * blocks_per_tile + r, 1, r * MOBA_BLOCK, True)

    def past(i, carry):
        visit(i * span, span, 0, False)
        return carry

    lax.fori_loop(0, qt * (blocks_per_tile // span), past, 0)

    out = None
    for hd in range(HEADS_PER_LANE_GROUP):
        acc = acc_sc[hd]
        o_h = acc[:, :LANES] / acc[:, LANES:]
        out = o_h if out is None else jnp.where(lane < hd * HEAD_DIM, out, o_h)
    o_ref[0] = out.astype(o_ref.dtype)


def _moba_prompt(q, kb, vb, means, tq):
    bsz, s, d = q.shape
    groups = d // LANES
    span = 2
    assert (tq // MOBA_BLOCK) % span == 0
    q_spec = pl.BlockSpec((1, tq, LANES), lambda b, g, i: (b, i, g))
    kv_spec = pl.BlockSpec((1, s, LANES), lambda b, g, i: (b, 0, g))
    mean_spec = pl.BlockSpec((1, LANES, LANES), lambda b, g, i: (b, g, 0))
    return pl.pallas_call(
        functools.partial(_moba_prompt_kernel, tq=tq, span=span),
        grid=(bsz, groups, s // tq),
        in_specs=[q_spec, kv_spec, kv_spec, mean_spec],
        out_specs=q_spec,
        out_shape=jax.ShapeDtypeStruct((bsz, s, d), BF16),
        scratch_shapes=[
            pltpu.VMEM((HEADS_PER_LANE_GROUP, tq, 2 * LANES), BF16),
            pltpu.VMEM((HEADS_PER_LANE_GROUP, tq, LANES), F32),
            pltpu.VMEM((HEADS_PER_LANE_GROUP, tq, 2 * LANES), F32),
        ],
        compiler_params=_params(3),
        name="moba_prompt",
    )(q, kb, vb, means)


def _mix_out_tail(x, c, gc, ga, attn, lng_ref, lnb_ref, wco_ref, wao_ref, wo_ref):
    mu = jnp.mean(c, axis=-1, keepdims=True)
    cc = c - mu
    var = jnp.mean(cc * cc, axis=-1, keepdims=True)
    y = cc * lax.rsqrt(var + LN_EPS) * lng_ref[...] + lnb_ref[...]
    act = (y * jax.nn.sigmoid(y)).astype(BF16)
    conv_branch = jnp.dot(act, wco_ref[...], preferred_element_type=F32)
    attn_branch = jnp.dot(attn.astype(BF16), wao_ref[...], preferred_element_type=F32)
    merged = jax.nn.sigmoid(gc) * conv_branch + jax.nn.sigmoid(ga) * attn_branch
    return x + jnp.dot(merged.astype(BF16), wo_ref[...], preferred_element_type=F32)


def _mix_out_seq_kernel(x_ref, glu_ref, halo_ref, hist_ref, gc_ref, ga_ref, attn_ref, cw_ref, cb_ref,
                        lng_ref, lnb_ref, wco_ref, wao_ref, wo_ref, o_ref, ext_sc, c_sc,
                        *, tiles_per_seq, row_chunk, col_chunk):
    tm, d = x_ref.shape
    first = (pl.program_id(0) % tiles_per_seq) == 0
    ext_sc[0:HALO, :] = jnp.where(first, hist_ref[0], halo_ref[...])
    ext_sc[HALO:HALO + tm, :] = glu_ref[...]
    for r0 in range(0, tm, row_chunk):
        for c0 in range(0, d, col_chunk):
            cols = slice(c0, c0 + col_chunk)
            acc = jnp.broadcast_to(cb_ref[:, cols], (row_chunk, col_chunk))
            for j in range(CONV_WIDTH):
                start = r0 + HALO_PAD + j
                acc = acc + cw_ref[j:j + 1, cols] * ext_sc[start:start + row_chunk, cols]
            c_sc[r0:r0 + row_chunk, cols] = acc
    o_ref[...] = _mix_out_tail(x_ref[...], c_sc[...], gc_ref[...], ga_ref[...], attn_ref[...],
                               lng_ref, lnb_ref, wco_ref, wao_ref, wo_ref)


def _mix_out_seq(x, glu, hist, gc, ga, attn, cw, cb, lng, lnb, wco, wao, wo, tm, seq_len):
    n, d = x.shape
    tiles_per_seq = seq_len // tm
    halo_per_tile = tm // HALO
    tile = pl.BlockSpec((tm, d), lambda i: (i, 0))
    halo = pl.BlockSpec((HALO, d), lambda i: (jnp.maximum(i * halo_per_tile - 1, 0), 0))
    hist_spec = pl.BlockSpec((1, HALO, d), lambda i: (i // tiles_per_seq, 0, 0))
    in_specs = [tile, tile, halo, hist_spec, tile, tile, tile,
                _resident(cw.shape), _resident((1, d)), _resident((1, d)), _resident((1, d)),
                _resident((d, d)), _resident((d, d)), _resident((d, d))]
    return pl.pallas_call(
        functools.partial(_mix_out_seq_kernel, tiles_per_seq=tiles_per_seq, row_chunk=32, col_chunk=512),
        grid=(n // tm,),
        in_specs=in_specs,
        out_specs=tile,
        out_shape=jax.ShapeDtypeStruct((n, d), F32),
        scratch_shapes=[pltpu.VMEM((HALO + tm, d), F32), pltpu.VMEM((tm, d), F32)],
        compiler_params=_params(1),
        name="mix_out_prompt",
    )(x, glu, glu, hist, gc, ga, attn, cw, cb, lng, lnb, wco, wao, wo)


def _mix_out_steps_kernel(x_ref, ext_ref, gc_ref, ga_ref, attn_ref, cw_ref, cb_ref,
                          lng_ref, lnb_ref, wco_ref, wao_ref, wo_ref, o_ref, c_sc):
    n_ext, n_seq, d = ext_ref.shape
    for t in range(n_ext - (CONV_WIDTH - 1)):
        acc = jnp.broadcast_to(cb_ref[...], (n_seq, d))
        for j in range(CONV_WIDTH):
            acc = acc + cw_ref[j:j + 1, :] * ext_ref[t + j]
        c_sc[t * n_seq:(t + 1) * n_seq, :] = acc
    o_ref[...] = _mix_out_tail(x_ref[...], c_sc[...], gc_ref[...], ga_ref[...], attn_ref[...],
                               lng_ref, lnb_ref, wco_ref, wao_ref, wo_ref)


def _mix_out_steps(x, ext, gc, ga, attn, cw, cb, lng, lnb, wco, wao, wo):
    n, d = x.shape
    full = pl.BlockSpec((n, d), lambda i: (0, 0))
    in_specs = [full, pl.BlockSpec(ext.shape, lambda i: (0, 0, 0)), full, full, full,
                _resident(cw.shape), _resident((1, d)), _resident((1, d)), _resident((1, d)),
                _resident((d, d)), _resident((d, d)), _resident((d, d))]
    return pl.pallas_call(
        _mix_out_steps_kernel,
        grid=(1,),
        in_specs=in_specs,
        out_specs=full,
        out_shape=jax.ShapeDtypeStruct((n, d), F32),
        scratch_shapes=[pltpu.VMEM((n, d), F32)],
        compiler_params=_params(1),
        name="mix_out_sample",
    )(x, ext, gc, ga, attn, cw, cb, lng, lnb, wco, wao, wo)


GATE_SLOTS = 4
SAMPLE_GROUP = 4


def _sample_gate_kernel(pt_ref, q_ref, ck_ref, top_ref, buf, mt_sc, sem, *, layer, n_blk):
    n_seq = q_ref.shape[0]
    d = q_ref.shape[2]
    total = n_seq * n_blk
    lane = lax.broadcasted_iota(jnp.int32, (d, LANES), 1)

    def copies(i, slot):
        b = i // n_blk
        n = i % n_blk
        return [pltpu.make_async_copy(ck_ref.at[layer, pt_ref[b, n * PAGES_PER_BLOCK + pg]],
                                      buf.at[slot, pg], sem.at[slot]) for pg in range(PAGES_PER_BLOCK)]

    for i in range(GATE_SLOTS - 1):
        for cp in copies(i, i):
            cp.start()
    mt_sc[...] = jnp.zeros(mt_sc.shape, F32)

    def gate_top_k(b):
        means_t = mt_sc[...] * (1.0 / MOBA_BLOCK)
        q_b = q_ref[b]
        channel = lax.broadcasted_iota(jnp.int32, q_b.shape, 1)
        q_heads = jnp.concatenate([jnp.where(channel // HEAD_DIM == hd, q_b, 0.0) for hd in range(N_HEADS)], axis=0)
        g = jnp.dot(q_heads, means_t, precision=lax.Precision.HIGHEST, preferred_element_type=F32)
        lane_t = lax.broadcasted_iota(jnp.int32, g.shape, 1)
        lane_f = lane_t.astype(F32)
        g = jnp.where(lane_t < n_blk, g, -jnp.inf)
        picks = jnp.zeros(g.shape, F32)
        for r in range(MOBA_TOP_K):
            mx = jnp.max(g, axis=1, keepdims=True)
            first = jnp.min(jnp.where(g == mx, lane_f, float(LANES)), axis=1, keepdims=True)
            picks = jnp.where(lane_t == r, first, picks)
            g = jnp.where(lane_f == first, -jnp.inf, g)
        top_ref[b] = picks.astype(jnp.int32)

    def body(i, carry):
        slot = i % GATE_SLOTS
        ahead = i + GATE_SLOTS - 1

        @pl.when(ahead < total)
        def _():
            for cp in copies(ahead, ahead % GATE_SLOTS):
                cp.start()

        for cp in copies(i, slot):
            cp.wait()
        block = buf[slot, 0]
        for pg in range(1, PAGES_PER_BLOCK):
            block = block + buf[slot, pg]
        key_sum = jnp.sum(block.reshape(d, PAGE_SIZE), axis=1, keepdims=True)
        n = i % n_blk
        mt_sc[...] = jnp.where(lane == n, key_sum, mt_sc[...])

        @pl.when(n == n_blk - 1)
        def _():
            gate_top_k(i // n_blk)

        return carry

    lax.fori_loop(0, total, body, 0)


def _sample_gate(page_table, q, cache_kt, layer):
    n_seq, rows, d = q.shape
    n_blk = page_table.shape[1] // PAGES_PER_BLOCK
    assert rows == SUBLANES and n_blk <= LANES and PAGE_SIZE == LANES
    grid_spec = pltpu.PrefetchScalarGridSpec(
        num_scalar_prefetch=1,
        grid=(1,),
        in_specs=[pl.BlockSpec(q.shape, lambda i, *_: (0, 0, 0)),
                  pl.BlockSpec(memory_space=pl.ANY)],
        out_specs=pl.BlockSpec((n_seq, N_HEADS * SUBLANES, LANES), lambda i, *_: (0, 0, 0)),
        scratch_shapes=[pltpu.VMEM((GATE_SLOTS, PAGES_PER_BLOCK, N_HEADS, HEAD_DIM, PAGE_SIZE), F32),
                        pltpu.VMEM((d, LANES), F32),
                        pltpu.SemaphoreType.DMA((GATE_SLOTS,))],
    )
    return pl.pallas_call(
        functools.partial(_sample_gate_kernel, layer=layer, n_blk=n_blk),
        grid_spec=grid_spec,
        out_shape=jax.ShapeDtypeStruct((n_seq, N_HEADS * SUBLANES, LANES), jnp.int32),
        compiler_params=_params(1),
        name="sample_gate",
    )(page_table, q, cache_kt)


def _sample_attn_kernel(top_ref, pt_ref, qt_ref, kn_ref, vn_ref, ck_ref, cv_ref, o_ref,
                        kbuf, vbuf, sem, *, layer, n_tok):
    b = pl.program_id(0)
    n_comb = n_tok * N_HEADS
    n_gather = MOBA_TOP_K * PAGES_PER_BLOCK
    scale = HEAD_DIM ** -0.5
    lane = lax.broadcasted_iota(jnp.int32, (HEAD_DIM, LANES), 1)
    lane_row = lax.broadcasted_iota(jnp.int32, (1, LANES), 1)

    def copies(group, slot):
        out = []
        for u in range(SAMPLE_GROUP):
            c = group * SAMPLE_GROUP + u
            hd = c % N_HEADS
            for r in range(MOBA_TOP_K):
                blk = top_ref[(b * n_comb + c) * MOBA_TOP_K + r]
                for pg in range(PAGES_PER_BLOCK):
                    phys = pt_ref[b, blk * PAGES_PER_BLOCK + pg]
                    dst = (u * MOBA_TOP_K + r) * PAGES_PER_BLOCK + pg
                    out.append(pltpu.make_async_copy(ck_ref.at[layer, phys, hd], kbuf.at[slot, dst], sem.at[0, slot]))
                    out.append(pltpu.make_async_copy(cv_ref.at[layer, phys, hd], vbuf.at[slot, dst], sem.at[1, slot]))
        return out

    def attend(c, slot, u):
        hd = c % N_HEADS
        t = c // N_HEADS
        base = u * n_gather
        q_col = jnp.sum(jnp.where(lane == c, qt_ref[0], 0.0), axis=1, keepdims=True) * scale
        logits = [jnp.sum(kbuf[slot, base + pg] * q_col, axis=0, keepdims=True) for pg in range(n_gather)]
        s_new = jnp.sum(kn_ref[0, hd] * q_col, axis=0, keepdims=True)
        logits.append(jnp.where(lane_row <= t, s_new, NEG_INF))
        row_max = functools.reduce(jnp.maximum, logits)
        m = jnp.max(row_max, axis=1, keepdims=True)
        probs = [jnp.exp(s - m) for s in logits]
        denom = jnp.sum(functools.reduce(jnp.add, probs), axis=1, keepdims=True)
        acc = probs[n_gather] * vn_ref[0, hd]
        for pg in range(n_gather):
            acc = acc + probs[pg] * vbuf[slot, base + pg]
        return jnp.sum(acc, axis=1, keepdims=True) / denom

    n_groups = n_comb // SAMPLE_GROUP
    for cp in copies(0, 0):
        cp.start()
    o_ref[0] = jnp.zeros((HEAD_DIM, LANES), F32)

    def body(group, carry):
        slot = group % 2

        @pl.when(group + 1 < n_groups)
        def _():
            for cp in copies(group + 1, 1 - slot):
                cp.start()

        for cp in copies(group, slot):
            cp.wait()
        out = o_ref[0]
        for u in range(SAMPLE_GROUP):
            c = group * SAMPLE_GROUP + u
            out = jnp.where(lane == c, attend(c, slot, u), out)
        o_ref[0] = out
        return carry

    lax.fori_loop(0, n_groups, body, 0)


def _sample_attn(top_flat, page_table, q_t, kn_t, vn_t, cache_kt, cache_vt, layer, n_tok):
    n_seq = q_t.shape[0]
    n_gather = MOBA_TOP_K * PAGES_PER_BLOCK
    assert n_tok * N_HEADS <= LANES and PAGE_SIZE == LANES and (n_tok * N_HEADS) % SAMPLE_GROUP == 0
    col_spec = pl.BlockSpec((1, HEAD_DIM, LANES), lambda b, *_: (b, 0, 0))
    new_spec = pl.BlockSpec((1, N_HEADS, HEAD_DIM, PAGE_SIZE), lambda b, *_: (b, 0, 0, 0))
    grid_spec = pltpu.PrefetchScalarGridSpec(
        num_scalar_prefetch=2,
        grid=(n_seq,),
        in_specs=[col_spec, new_spec, new_spec,
                  pl.BlockSpec(memory_space=pl.ANY),
                  pl.BlockSpec(memory_space=pl.ANY)],
        out_specs=col_spec,
        scratch_shapes=[pltpu.VMEM((2, SAMPLE_GROUP * n_gather, HEAD_DIM, PAGE_SIZE), F32),
                        pltpu.VMEM((2, SAMPLE_GROUP * n_gather, HEAD_DIM, PAGE_SIZE), F32),
                        pltpu.SemaphoreType.DMA((2, 2))],
    )
    return pl.pallas_call(
        functools.partial(_sample_attn_kernel, layer=layer, n_tok=n_tok),
        grid_spec=grid_spec,
        out_shape=jax.ShapeDtypeStruct((n_seq, HEAD_DIM, LANES), F32),
        compiler_params=_params(1),
        name="sample_attn",
    )(top_flat, page_table, q_t, kn_t, vn_t, cache_kt, cache_vt)


def _rope_tables(pos):
    inv = ROPE_THETA ** (-jnp.arange(0, HEAD_DIM, 2, dtype=F32) / HEAD_DIM)
    ang = pos.astype(F32)[:, None] * inv[None, :]
    cos, sin = jnp.cos(ang), jnp.sin(ang)
    cos_g = jnp.concatenate([cos, cos] * HEADS_PER_LANE_GROUP, axis=1)
    sin_g = jnp.concatenate([-sin, sin] * HEADS_PER_LANE_GROUP, axis=1)
    return cos_g, sin_g


def kernel(x_prompt, x_sample, cache_k, cache_v, cache_conv, page_table,
           ffn1_norm, ffn1_w_gate, ffn1_w_up, ffn1_w_down,
           mix_norm, w_in, conv_dw_w, conv_dw_b, conv_ln_g, conv_ln_b,
           w_conv_out, w_attn_out, w_out,
           ffn2_norm, ffn2_w_gate, ffn2_w_up, ffn2_w_down, final_norm):
    bsz, seq, d = x_prompt.shape
    n_seq, n_tok, _ = x_sample.shape
    depth = w_in.shape[0]
    n_pages = page_table.shape[1]
    past = n_pages * PAGE_SIZE
    assert d == N_HEADS * HEAD_DIM and past % MOBA_BLOCK == 0 and n_pages // PAGES_PER_BLOCK >= MOBA_TOP_K
    assert seq % MOBA_BLOCK == 0 and seq >= CONV_WIDTH - 1 and seq // MOBA_BLOCK <= LANES

    tm_ffn, tm_mix, tq = 512, 256, 1024
    n_p, n_s = bsz * seq, n_seq * n_tok

    bf = lambda w: w.astype(BF16)
    row = lambda g: g.reshape(1, -1)

    cos_p, sin_p = _rope_tables(jnp.arange(seq))
    cos_s, sin_s = _rope_tables(past + jnp.repeat(jnp.arange(n_tok), n_seq))

    xp = x_prompt.reshape(n_p, d)
    xs = x_sample.transpose(1, 0, 2).reshape(n_s, d)
    cache_kt = jnp.swapaxes(cache_k, 3, 4)
    cache_vt = jnp.swapaxes(cache_v, 3, 4)
    zero_hist = jnp.zeros((bsz, HALO, d), F32)
    n_comb = n_tok * N_HEADS
    assert n_tok <= SUBLANES

    kp_out, vp_out, convp_out, ks_out, vs_out, convs_out = [], [], [], [], [], []
    for l in range(depth):
        last = l == depth - 1
        f1 = (row(ffn1_norm[l]), bf(ffn1_w_gate[l]), bf(ffn1_w_up[l]), bf(ffn1_w_down[l]))
        f2 = (row(ffn2_norm[l]), bf(ffn2_w_gate[l]), bf(ffn2_w_up[l]), bf(ffn2_w_down[l]))
        fin = row(final_norm) if last else None
        w_in_l = bf(w_in[l])
        mix_w = (conv_dw_w[l], row(conv_dw_b[l]), row(conv_ln_g[l]), row(conv_ln_b[l]),
                 bf(w_conv_out[l]), bf(w_attn_out[l]), bf(w_out[l]))

        xp = _ffn(xp, *f1, None, tm_ffn)
        glu, gc, ga, q, kb, vb, kpg, vpg, kmean = _mix_in(xp, row(mix_norm[l]), w_in_l, cos_p, sin_p, tm_mix, True)
        n_blk = seq // MOBA_BLOCK
        means = jnp.pad(kmean.reshape(bsz, n_blk, d), ((0, 0), (0, LANES - n_blk), (0, 0))).transpose(0, 2, 1)
        attn = _moba_prompt(q.reshape(bsz, seq, d), kb.reshape(bsz, seq, d), vb.reshape(bsz, seq, d), means, tq)
        xp = _mix_out_seq(xp, glu, zero_hist, gc, ga, attn.reshape(n_p, d), *mix_w, tm_mix, seq)
        xp = _ffn(xp, *f2, fin, tm_ffn)
        kp_out.append(kpg.reshape(bsz, seq // PAGE_SIZE, N_HEADS, HEAD_DIM, PAGE_SIZE))
        vp_out.append(vpg.reshape(bsz, seq // PAGE_SIZE, N_HEADS, HEAD_DIM, PAGE_SIZE))
        convp_out.append(glu.reshape(bsz, seq, d)[:, seq - (CONV_WIDTH - 1):])

        xs = _ffn(xs, *f1, None, n_s)
        glu_s, gc_s, ga_s, q_s, k_s, v_s = _mix_in(xs, row(mix_norm[l]), w_in_l, cos_s, sin_s, n_s, False)
        to_seq_major = lambda a: a.reshape(n_tok, n_seq, d).transpose(1, 0, 2)
        k_new = to_seq_major(k_s).reshape(n_seq, n_tok, N_HEADS, HEAD_DIM).transpose(0, 2, 1, 3)
        v_new = to_seq_major(v_s).reshape(n_seq, n_tok, N_HEADS, HEAD_DIM).transpose(0, 2, 1, 3)
        q_b = to_seq_major(q_s)
        top = _sample_gate(page_table, jnp.pad(q_b, ((0, 0), (0, SUBLANES - n_tok), (0, 0))), cache_kt, l)
        top = top.reshape(n_seq, N_HEADS, SUBLANES, LANES)[:, :, :n_tok, :MOBA_TOP_K]
        top_flat = top.transpose(0, 2, 1, 3).reshape(-1)
        lane_pad = lambda a: jnp.pad(a, [(0, 0)] * (a.ndim - 1) + [(0, LANES - a.shape[-1])])
        q_t = lane_pad(q_b.reshape(n_seq, n_comb, HEAD_DIM).transpose(0, 2, 1))
        attn_s = _sample_attn(top_flat, page_table, q_t, lane_pad(k_new.transpose(0, 1, 3, 2)),
                              lane_pad(v_new.transpose(0, 1, 3, 2)), cache_kt, cache_vt, l, n_tok)
        attn_s = attn_s[:, :, :n_comb].transpose(2, 0, 1).reshape(n_tok, N_HEADS, n_seq, HEAD_DIM)
        attn_s = attn_s.transpose(0, 2, 1, 3).reshape(n_s, d)
        ext = jnp.concatenate([cache_conv[l].transpose(1, 0, 2), glu_s.reshape(n_tok, n_seq, d)], axis=0)
        xs = _mix_out_steps(xs, ext, gc_s, ga_s, attn_s, *mix_w)
        xs = _ffn(xs, *f2, fin, n_s)
        ks_out.append(k_new)
        vs_out.append(v_new)
        convs_out.append(jnp.concatenate([cache_conv[l][:, n_tok:], to_seq_major(glu_s)], axis=1))

    y_prompt = xp.reshape(bsz, seq, d)
    y_sample = xs.reshape(n_tok, n_seq, d).transpose(1, 0, 2)
    pages = lambda parts: jnp.swapaxes(jnp.stack(parts), -1, -2)
    return (y_prompt, y_sample, pages(kp_out), pages(vp_out), jnp.stack(convp_out),
            jnp.stack(ks_out), jnp.stack(vs_out), jnp.stack(convs_out))
```

```python
import functools

import jax
import jax.numpy as jnp
from jax import lax
from jax.experimental import pallas as pl
from jax.experimental.pallas import tpu as pltpu

F32 = jnp.float32
BF16 = jnp.bfloat16

N_HEADS = 16
HEAD_DIM = 64
CONV_WIDTH = 31
MOBA_BLOCK = 256
MOBA_TOP_K = 3
PAGE_SIZE = 128
ROPE_THETA = 10000.0
RMS_EPS = 1e-6
LN_EPS = 1e-5
FFN_RES = 0.5
NEG_INF = -1e30

LANES = 128
SUBLANES = 8
VMEM_LIMIT = 56 * 1024 * 1024

HEADS_PER_LANE_GROUP = LANES // HEAD_DIM
PAGES_PER_BLOCK = MOBA_BLOCK // PAGE_SIZE
HALO = 32
HALO_PAD = HALO - (CONV_WIDTH - 1)
MASK_NEG = -(2.0 ** 100)


def _resident(shape):
    nd = len(shape)
    return pl.BlockSpec(shape, lambda *_: (0,) * nd, pipeline_mode=pl.Buffered(1))


def _params(n_grid):
    return pltpu.CompilerParams(dimension_semantics=("arbitrary",) * n_grid, vmem_limit_bytes=VMEM_LIMIT)


def _rms(x, g):
    return x * lax.rsqrt(jnp.mean(x * x, axis=-1, keepdims=True) + RMS_EPS) * g


def _ffn_kernel(x_ref, g_ref, wg_ref, wu_ref, wd_ref, *rest, ff_chunk, final):
    if final:
        fn_ref, o_ref = rest
    else:
        (o_ref,) = rest
    x = x_ref[...]
    h = _rms(x, g_ref[...]).astype(BF16)
    d_ff = wg_ref.shape[1]
    y = None
    for c0 in range(0, d_ff, ff_chunk):
        gate = jnp.dot(h, wg_ref[:, c0:c0 + ff_chunk], preferred_element_type=F32)
        up = jnp.dot(h, wu_ref[:, c0:c0 + ff_chunk], preferred_element_type=F32)
        a = (gate * jax.nn.sigmoid(gate) * up).astype(BF16)
        part = jnp.dot(a, wd_ref[c0:c0 + ff_chunk, :], preferred_element_type=F32)
        y = part if y is None else y + part
    out = x + FFN_RES * y
    if final:
        out = _rms(out, fn_ref[...])
    o_ref[...] = out


def _ffn(x, norm_g, wg, wu, wd, final_g, tm):
    n, d = x.shape
    d_ff = wg.shape[1]
    ff_chunk = d_ff // 2 if (d_ff // 2) % LANES == 0 else d_ff
    final = final_g is not None
    tile = pl.BlockSpec((tm, d), lambda i: (i, 0))
    in_specs = [tile, _resident((1, d)), _resident((d, d_ff)), _resident((d, d_ff)), _resident((d_ff, d))]
    args = [x, norm_g, wg, wu, wd]
    if final:
        in_specs.append(_resident((1, d)))
        args.append(final_g)
    return pl.pallas_call(
        functools.partial(_ffn_kernel, ff_chunk=ff_chunk, final=final),
        grid=(n // tm,),
        in_specs=in_specs,
        out_specs=tile,
        out_shape=jax.ShapeDtypeStruct((n, d), F32),
        compiler_params=_params(1),
        name="ffn",
    )(*args)


def _rope(x, cos, sin_signed):
    tm, d = x.shape
    lane = lax.broadcasted_iota(jnp.int32, (tm, LANES), 1)
    first_half = (lane % HEAD_DIM) < (HEAD_DIM // 2)
    outs = []
    for g in range(d // LANES):
        xs = x[:, g * LANES:(g + 1) * LANES]
        partner = jnp.where(first_half,
                            pltpu.roll(xs, LANES - HEAD_DIM // 2, 1),
                            pltpu.roll(xs, HEAD_DIM // 2, 1))
        outs.append(xs * cos + partner * sin_signed)
    return jnp.concatenate(outs, axis=1)


def _mix_in_kernel(x_ref, g_ref, w_ref, cos_ref, sin_ref, glu_ref, gc_ref, ga_ref, q_ref, *rest, paged):
    d = x_ref.shape[1]
    tm = x_ref.shape[0]
    h = _rms(x_ref[...], g_ref[...]).astype(BF16)

    def proj(idx):
        return jnp.dot(h, w_ref[:, idx * d:(idx + 1) * d], preferred_element_type=F32)

    glu_ref[...] = proj(0) * jax.nn.sigmoid(proj(1))
    cos = cos_ref[...]
    sin_signed = sin_ref[...]
    q_ref[...] = _rope(proj(2), cos, sin_signed)
    k = _rope(proj(3), cos, sin_signed)
    v = proj(4)
    gc_ref[...] = proj(5)
    ga_ref[...] = proj(6)
    if paged:
        kb_ref, vb_ref, kp_ref, vp_ref, km_ref = rest
        kb_ref[...] = k.astype(BF16)
        vb_ref[...] = v.astype(BF16)
        k_t = k.T
        v_t = v.T
        for p in range(tm // PAGE_SIZE):
            toks = slice(p * PAGE_SIZE, (p + 1) * PAGE_SIZE)
            for hd in range(N_HEADS):
                dims = slice(hd * HEAD_DIM, (hd + 1) * HEAD_DIM)
                kp_ref[p, hd] = k_t[dims, toks]
                vp_ref[p, hd] = v_t[dims, toks]
        means = [jnp.mean(k[b * MOBA_BLOCK:(b + 1) * MOBA_BLOCK], axis=0, keepdims=True)
                 for b in range(tm // MOBA_BLOCK)]
        km_ref[0] = jnp.concatenate(means, axis=0)
    else:
        kf_ref, vf_ref = rest
        kf_ref[...] = k
        vf_ref[...] = v


def _mix_in(x, norm_g, w_in, cos, sin_signed, tm, paged):
    n, d = x.shape
    n_pos_tiles = cos.shape[0] // tm
    tile = pl.BlockSpec((tm, d), lambda i: (i, 0))
    table = pl.BlockSpec((tm, LANES), lambda i: (i % n_pos_tiles, 0))
    in_specs = [tile, _resident((1, d)), _resident(w_in.shape), table, table]
    out_specs = [tile] * 4
    out_shape = [jax.ShapeDtypeStruct((n, d), F32)] * 4
    if paged:
        pages = tm // PAGE_SIZE
        blocks = tm // MOBA_BLOCK
        page_spec = pl.BlockSpec((pages, N_HEADS, HEAD_DIM, PAGE_SIZE), lambda i: (i, 0, 0, 0))
        page_shape = jax.ShapeDtypeStruct((n // PAGE_SIZE, N_HEADS, HEAD_DIM, PAGE_SIZE), F32)
        out_specs += [tile, tile, page_spec, page_spec, pl.BlockSpec((1, blocks, d), lambda i: (i, 0, 0))]
        out_shape += [jax.ShapeDtypeStruct((n, d), BF16)] * 2 + [page_shape] * 2
        out_shape += [jax.ShapeDtypeStruct((n // tm, blocks, d), F32)]
    else:
        out_specs += [tile, tile]
        out_shape += [jax.ShapeDtypeStruct((n, d), F32)] * 2
    return pl.pallas_call(
        functools.partial(_mix_in_kernel, paged=paged),
        grid=(n // tm,),
        in_specs=in_specs,
        out_specs=out_specs,
        out_shape=out_shape,
        compiler_params=_params(1),
        name="mix_in_prompt" if paged else "mix_in_sample",
    )(x, norm_g, w_in, cos, sin_signed)


def _selection_bias(gate, valid, own, lane_f):
    g = jnp.where(valid, gate, -jnp.inf)
    bias = jnp.where(own, 0.0, MASK_NEG)
    for _ in range(MOBA_TOP_K):
        mx = jnp.max(g, axis=1, keepdims=True)
        first = jnp.min(jnp.where(g == mx, lane_f, float(LANES)), axis=1, keepdims=True)
        pick = lane_f == first
        bias = jnp.where(pick, jnp.where(valid, 0.0, bias), bias)
        g = jnp.where(pick, -jnp.inf, g)
    return bias


def _moba_prompt_kernel(q_ref, k_ref, v_ref, mean_ref, o_ref, qext_sc, m_sc, acc_sc, *, tq, span):
    qt = pl.program_id(2)
    blocks_per_tile = tq // MOBA_BLOCK
    q = q_ref[0]
    lane = lax.broadcasted_iota(jnp.int32, (tq, LANES), 1)
    row = lax.broadcasted_iota(jnp.int32, (tq, LANES), 0)
    q_blk = qt * blocks_per_tile + row // MOBA_BLOCK
    lane_f = lane.astype(F32)
    means_t = mean_ref[0]
    scale = HEAD_DIM ** -0.5

    for hd in range(HEADS_PER_LANE_GROUP):
        in_head = (lane >= hd * HEAD_DIM) & (lane < (hd + 1) * HEAD_DIM)
        qm = jnp.where(in_head, q, 0.0)
        gate = jnp.dot(qm, means_t, precision=lax.Precision.HIGHEST, preferred_element_type=F32)
        bias = _selection_bias(gate, lane < q_blk, lane == q_blk, lane_f)
        qext_sc[hd] = jnp.concatenate([(qm * scale).astype(BF16), bias.astype(BF16)], axis=1)
        m_sc[hd] = jnp.full((tq, LANES), -jnp.inf, F32)
        acc_sc[hd] = jnp.zeros((tq, 2 * LANES), F32)

    def visit(j0, n_blk, row0, causal):
        n_keys = n_blk * MOBA_BLOCK
        start = pl.multiple_of(j0 * MOBA_BLOCK, MOBA_BLOCK)
        kj = k_ref[0, pl.ds(start, n_keys), :]
        vj = v_ref[0, pl.ds(start, n_keys), :]
        key_lane = lax.broadcasted_iota(jnp.int32, (n_keys, LANES), 1)
        key_blk = j0 + lax.broadcasted_iota(jnp.int32, (n_keys, LANES), 0) // MOBA_BLOCK
        k_ext = jnp.concatenate([kj, jnp.where(key_lane == key_blk, 1.0, 0.0).astype(BF16)], axis=1)
        v_ext = jnp.concatenate([vj, jnp.ones((n_keys, LANES), BF16)], axis=1)
        rows = slice(row0, tq)
        n_rows = tq - row0
        for hd in range(HEADS_PER_LANE_GROUP):
            s = lax.dot_general(qext_sc[hd, rows, :], k_ext, (((1,), (1,)), ((), ())),
                                preferred_element_type=F32)
            if causal:
                r_i = lax.broadcasted_iota(jnp.int32, (n_rows, n_keys), 0)
                c_i = lax.broadcasted_iota(jnp.int32, (n_rows, n_keys), 1)
                s = jnp.where(c_i <= r_i, s, MASK_NEG)
            m_prev = m_sc[hd, rows, :]
            m_next = jnp.maximum(m_prev, jnp.max(s, axis=1, keepdims=True))
            alpha = jnp.exp(m_prev - m_next)
            p = jnp.exp((s - jnp.concatenate([m_next] * (n_keys // LANES), axis=1)).astype(BF16))
            pv = jnp.dot(p, v_ext, preferred_element_type=F32)
            acc_sc[hd, rows, :] = jnp.concatenate([alpha, alpha], axis=1) * acc_sc[hd, rows, :] + pv
            m_sc[hd, rows, :] = m_next

    for r in range(blocks_per_tile):
        visit(qt * blocks_per_tile + r, 1, r * MOBA_BLOCK, True)

    def past(i, carry):
        visit(i * span, span, 0, False)
        return carry

    lax.fori_loop(0, qt * (blocks_per_tile // span), past, 0)

    out = None
    for hd in range(HEADS_PER_LANE_GROUP):
        acc = acc_sc[hd]
        o_h = acc[:, :LANES] / acc[:, LANES:]
        out = o_h if out is None else jnp.where(lane < hd * HEAD_DIM, out, o_h)
    o_ref[0] = out.astype(o_ref.dtype)


def _moba_prompt(q, kb, vb, means, tq):
    bsz, s, d = q.shape
    groups = d // LANES
    span = 4
    assert (tq // MOBA_BLOCK) % span == 0
    q_spec = pl.BlockSpec((1, tq, LANES), lambda b, g, i: (b, i, g))
    kv_spec = pl.BlockSpec((1, s, LANES), lambda b, g, i: (b, 0, g))
    mean_spec = pl.BlockSpec((1, LANES, LANES), lambda b, g, i: (b, g, 0))
    return pl.pallas_call(
        functools.partial(_moba_prompt_kernel, tq=tq, span=span),
        grid=(bsz, groups, s // tq),
        in_specs=[q_spec, kv_spec, kv_spec, mean_spec],
        out_specs=q_spec,
        out_shape=jax.ShapeDtypeStruct((bsz, s, d), BF16),
        scratch_shapes=[
            pltpu.VMEM((HEADS_PER_LANE_GROUP, tq, 2 * LANES), BF16),
            pltpu.VMEM((HEADS_PER_LANE_GROUP, tq, LANES), F32),
            pltpu.VMEM((HEADS_PER_LANE_GROUP, tq, 2 * LANES), F32),
        ],
        compiler_params=_params(3),
        name="moba_prompt",
    )(q, kb, vb, means)


def _mix_out_tail(x, c, gc, ga, attn, lng_ref, lnb_ref, wco_ref, wao_ref, wo_ref):
    mu = jnp.mean(c, axis=-1, keepdims=True)
    cc = c - mu
    var = jnp.mean(cc * cc, axis=-1, keepdims=True)
    y = cc * lax.rsqrt(var + LN_EPS) * lng_ref[...] + lnb_ref[...]
    act = (y * jax.nn.sigmoid(y)).astype(BF16)
    conv_branch = jnp.dot(act, wco_ref[...], preferred_element_type=F32)
    attn_branch = jnp.dot(attn.astype(BF16), wao_ref[...], preferred_element_type=F32)
    merged = jax.nn.sigmoid(gc) * conv_branch + jax.nn.sigmoid(ga) * attn_branch
    return x + jnp.dot(merged.astype(BF16), wo_ref[...], preferred_element_type=F32)


def _mix_out_seq_kernel(x_ref, glu_ref, halo_ref, hist_ref, gc_ref, ga_ref, attn_ref, cw_ref, cb_ref,
                        lng_ref, lnb_ref, wco_ref, wao_ref, wo_ref, o_ref, ext_sc, phase_sc, c_sc,
                        *, tiles_per_seq, row_chunk, col_chunk):
    tm, d = x_ref.shape
    first = (pl.program_id(0) % tiles_per_seq) == 0
    ext_sc[0:HALO, :] = jnp.where(first, hist_ref[0], halo_ref[...])
    ext_sc[HALO:HALO + tm, :] = glu_ref[...]
    phase_rows = phase_sc.shape[1]
    for r in range(1, SUBLANES):
        phase_sc[r - 1] = ext_sc[r:r + phase_rows, :]
    for r0 in range(0, tm, row_chunk):
        for c0 in range(0, d, col_chunk):
            cols = slice(c0, c0 + col_chunk)
            acc = jnp.broadcast_to(cb_ref[:, cols], (row_chunk, col_chunk))
            for j in range(CONV_WIDTH):
                aligned, r = divmod(HALO_PAD + j, SUBLANES)
                start = r0 + aligned * SUBLANES
                if r == 0:
                    taps = ext_sc[start:start + row_chunk, cols]
                else:
                    taps = phase_sc[r - 1, start:start + row_chunk, cols]
                acc = acc + cw_ref[j:j + 1, cols] * taps
            c_sc[r0:r0 + row_chunk, cols] = acc
    o_ref[...] = _mix_out_tail(x_ref[...], c_sc[...], gc_ref[...], ga_ref[...], attn_ref[...],
                               lng_ref, lnb_ref, wco_ref, wao_ref, wo_ref)


def _mix_out_seq(x, glu, hist, gc, ga, attn, cw, cb, lng, lnb, wco, wao, wo, tm, seq_len):
    n, d = x.shape
    tiles_per_seq = seq_len // tm
    halo_per_tile = tm // HALO
    tile = pl.BlockSpec((tm, d), lambda i: (i, 0))
    halo = pl.BlockSpec((HALO, d), lambda i: (jnp.maximum(i * halo_per_tile - 1, 0), 0))
    hist_spec = pl.BlockSpec((1, HALO, d), lambda i: (i // tiles_per_seq, 0, 0))
    in_specs = [tile, tile, halo, hist_spec, tile, tile, tile,
                _resident(cw.shape), _resident((1, d)), _resident((1, d)), _resident((1, d)),
                _resident((d, d)), _resident((d, d)), _resident((d, d))]
    return pl.pallas_call(
        functools.partial(_mix_out_seq_kernel, tiles_per_seq=tiles_per_seq, row_chunk=32, col_chunk=512),
        grid=(n // tm,),
        in_specs=in_specs,
        out_specs=tile,
        out_shape=jax.ShapeDtypeStruct((n, d), F32),
        scratch_shapes=[pltpu.VMEM((HALO + tm, d), F32),
                        pltpu.VMEM((SUBLANES - 1, HALO + tm - SUBLANES, d), F32),
                        pltpu.VMEM((tm, d), F32)],
        compiler_params=_params(1),
        name="mix_out_prompt",
    )(x, glu, glu, hist, gc, ga, attn, cw, cb, lng, lnb, wco, wao, wo)


def _mix_out_steps_kernel(x_ref, ext_ref, gc_ref, ga_ref, attn_ref, cw_ref, cb_ref,
                          lng_ref, lnb_ref, wco_ref, wao_ref, wo_ref, o_ref, c_sc):
    n_ext, n_seq, d = ext_ref.shape
    for t in range(n_ext - (CONV_WIDTH - 1)):
        acc = jnp.broadcast_to(cb_ref[...], (n_seq, d))
        for j in range(CONV_WIDTH):
            acc = acc + cw_ref[j:j + 1, :] * ext_ref[t + j]
        c_sc[t * n_seq:(t + 1) * n_seq, :] = acc
    o_ref[...] = _mix_out_tail(x_ref[...], c_sc[...], gc_ref[...], ga_ref[...], attn_ref[...],
                               lng_ref, lnb_ref, wco_ref, wao_ref, wo_ref)


def _mix_out_steps(x, ext, gc, ga, attn, cw, cb, lng, lnb, wco, wao, wo):
    n, d = x.shape
    full = pl.BlockSpec((n, d), lambda i: (0, 0))
    in_specs = [full, pl.BlockSpec(ext.shape, lambda i: (0, 0, 0)), full, full, full,
                _resident(cw.shape), _resident((1, d)), _resident((1, d)), _resident((1, d)),
                _resident((d, d)), _resident((d, d)), _resident((d, d))]
    return pl.pallas_call(
        _mix_out_steps_kernel,
        grid=(1,),
        in_specs=in_specs,
        out_specs=full,
        out_shape=jax.ShapeDtypeStruct((n, d), F32),
        scratch_shapes=[pltpu.VMEM((n, d), F32)],
        compiler_params=_params(1),
        name="mix_out_sample",
    )(x, ext, gc, ga, attn, cw, cb, lng, lnb, wco, wao, wo)


GATE_SLOTS = 8
SAMPLE_SLOTS = 4
SAMPLE_GROUP = 4


def _sample_gate_kernel(pt_ref, q_ref, ck_ref, top_ref, buf, mt_sc, sem, *, layer, n_blk):
    n_seq = q_ref.shape[0]
    d = q_ref.shape[2]
    total = n_seq * n_blk
    lane = lax.broadcasted_iota(jnp.int32, (d, LANES), 1)

    def copies(i, slot):
        b = i // n_blk
        n = i % n_blk
        return [pltpu.make_async_copy(ck_ref.at[layer, pt_ref[b, n * PAGES_PER_BLOCK + pg]],
                                      buf.at[slot, pg], sem.at[slot]) for pg in range(PAGES_PER_BLOCK)]

    for i in range(GATE_SLOTS - 1):
        for cp in copies(i, i):
            cp.start()
    mt_sc[...] = jnp.zeros(mt_sc.shape, F32)

    def gate_top_k(b):
        means_t = mt_sc[...] * (1.0 / MOBA_BLOCK)
        q_b = q_ref[b]
        channel = lax.broadcasted_iota(jnp.int32, q_b.shape, 1)
        q_heads = jnp.concatenate([jnp.where(channel // HEAD_DIM == hd, q_b, 0.0) for hd in range(N_HEADS)], axis=0)
        g = jnp.dot(q_heads, means_t, precision=lax.Precision.HIGHEST, preferred_element_type=F32)
        lane_t = lax.broadcasted_iota(jnp.int32, g.shape, 1)
        lane_f = lane_t.astype(F32)
        g = jnp.where(lane_t < n_blk, g, -jnp.inf)
        picks = jnp.zeros(g.shape, F32)
        for r in range(MOBA_TOP_K):
            mx = jnp.max(g, axis=1, keepdims=True)
            first = jnp.min(jnp.where(g == mx, lane_f, float(LANES)), axis=1, keepdims=True)
            picks = jnp.where(lane_t == r, first, picks)
            g = jnp.where(lane_f == first, -jnp.inf, g)
        top_ref[b] = picks.astype(jnp.int32)

    def body(i, carry):
        slot = i % GATE_SLOTS
        ahead = i + GATE_SLOTS - 1

        @pl.when(ahead < total)
        def _():
            for cp in copies(ahead, ahead % GATE_SLOTS):
                cp.start()

        for cp in copies(i, slot):
            cp.wait()
        block = buf[slot, 0]
        for pg in range(1, PAGES_PER_BLOCK):
            block = block + buf[slot, pg]
        key_sum = jnp.sum(block.reshape(d, PAGE_SIZE), axis=1, keepdims=True)
        n = i % n_blk
        mt_sc[...] = jnp.where(lane == n, key_sum, mt_sc[...])

        @pl.when(n == n_blk - 1)
        def _():
            gate_top_k(i // n_blk)

        return carry

    lax.fori_loop(0, total, body, 0)


def _sample_gate(page_table, q, cache_kt, layer):
    n_seq, rows, d = q.shape
    n_blk = page_table.shape[1] // PAGES_PER_BLOCK
    assert rows == SUBLANES and n_blk <= LANES and PAGE_SIZE == LANES
    grid_spec = pltpu.PrefetchScalarGridSpec(
        num_scalar_prefetch=1,
        grid=(1,),
        in_specs=[pl.BlockSpec(q.shape, lambda i, *_: (0, 0, 0)),
                  pl.BlockSpec(memory_space=pl.ANY)],
        out_specs=pl.BlockSpec((n_seq, N_HEADS * SUBLANES, LANES), lambda i, *_: (0, 0, 0)),
        scratch_shapes=[pltpu.VMEM((GATE_SLOTS, PAGES_PER_BLOCK, N_HEADS, HEAD_DIM, PAGE_SIZE), F32),
                        pltpu.VMEM((d, LANES), F32),
                        pltpu.SemaphoreType.DMA((GATE_SLOTS,))],
    )
    return pl.pallas_call(
        functools.partial(_sample_gate_kernel, layer=layer, n_blk=n_blk),
        grid_spec=grid_spec,
        out_shape=jax.ShapeDtypeStruct((n_seq, N_HEADS * SUBLANES, LANES), jnp.int32),
        compiler_params=_params(1),
        name="sample_gate",
    )(page_table, q, cache_kt)


def _sample_attn_kernel(top_ref, pt_ref, qt_ref, kn_ref, vn_ref, ck_ref, cv_ref, o_ref,
                        kbuf, vbuf, sem, *, layer, n_tok):
    b = pl.program_id(0)
    n_comb = n_tok * N_HEADS
    n_gather = MOBA_TOP_K * PAGES_PER_BLOCK
    scale = HEAD_DIM ** -0.5
    lane = lax.broadcasted_iota(jnp.int32, (HEAD_DIM, LANES), 1)
    lane_row = lax.broadcasted_iota(jnp.int32, (1, LANES), 1)

    def copies(group, slot):
        out = []
        for u in range(SAMPLE_GROUP):
            c = group * SAMPLE_GROUP + u
            hd = c % N_HEADS
            for r in range(MOBA_TOP_K):
                blk = top_ref[(b * n_comb + c) * MOBA_TOP_K + r]
                for pg in range(PAGES_PER_BLOCK):
                    phys = pt_ref[b, blk * PAGES_PER_BLOCK + pg]
                    dst = (u * MOBA_TOP_K + r) * PAGES_PER_BLOCK + pg
                    out.append(pltpu.make_async_copy(ck_ref.at[layer, phys, hd], kbuf.at[slot, dst], sem.at[0, slot]))
                    out.append(pltpu.make_async_copy(cv_ref.at[layer, phys, hd], vbuf.at[slot, dst], sem.at[1, slot]))
        return out

    def attend(c, slot, u):
        hd = c % N_HEADS
        t = c // N_HEADS
        base = u * n_gather
        q_col = jnp.sum(jnp.where(lane == c, qt_ref[0], 0.0), axis=1, keepdims=True) * scale
        logits = [jnp.sum(kbuf[slot, base + pg] * q_col, axis=0, keepdims=True) for pg in range(n_gather)]
        s_new = jnp.sum(kn_ref[0, hd] * q_col, axis=0, keepdims=True)
        logits.append(jnp.where(lane_row <= t, s_new, NEG_INF))
        row_max = functools.reduce(jnp.maximum, logits)
        m = jnp.max(row_max, axis=1, keepdims=True)
        probs = [jnp.exp(s - m) for s in logits]
        denom = jnp.sum(functools.reduce(jnp.add, probs), axis=1, keepdims=True)
        acc = probs[n_gather] * vn_ref[0, hd]
        for pg in range(n_gather):
            acc = acc + probs[pg] * vbuf[slot, base + pg]
        return jnp.sum(acc, axis=1, keepdims=True) / denom

    n_groups = n_comb // SAMPLE_GROUP
    for g in range(SAMPLE_SLOTS - 1):
        for cp in copies(g, g):
            cp.start()
    o_ref[0] = jnp.zeros((HEAD_DIM, LANES), F32)

    def body(group, carry):
        slot = group % SAMPLE_SLOTS
        ahead = group + SAMPLE_SLOTS - 1

        @pl.when(ahead < n_groups)
        def _():
            for cp in copies(ahead, ahead % SAMPLE_SLOTS):
                cp.start()

        for cp in copies(group, slot):
            cp.wait()
        out = o_ref[0]
        for u in range(SAMPLE_GROUP):
            c = group * SAMPLE_GROUP + u
            out = jnp.where(lane == c, attend(c, slot, u), out)
        o_ref[0] = out
        return carry

    lax.fori_loop(0, n_groups, body, 0)


def _sample_attn(top_flat, page_table, q_t, kn_t, vn_t, cache_kt, cache_vt, layer, n_tok):
    n_seq = q_t.shape[0]
    n_gather = MOBA_TOP_K * PAGES_PER_BLOCK
    assert n_tok * N_HEADS <= LANES and PAGE_SIZE == LANES and (n_tok * N_HEADS) % SAMPLE_GROUP == 0
    assert n_tok * N_HEADS // SAMPLE_GROUP >= SAMPLE_SLOTS - 1
    col_spec = pl.BlockSpec((1, HEAD_DIM, LANES), lambda b, *_: (b, 0, 0))
    new_spec = pl.BlockSpec((1, N_HEADS, HEAD_DIM, PAGE_SIZE), lambda b, *_: (b, 0, 0, 0))
    grid_spec = pltpu.PrefetchScalarGridSpec(
        num_scalar_prefetch=2,
        grid=(n_seq,),
        in_specs=[col_spec, new_spec, new_spec,
                  pl.BlockSpec(memory_space=pl.ANY),
                  pl.BlockSpec(memory_space=pl.ANY)],
        out_specs=col_spec,
        scratch_shapes=[pltpu.VMEM((SAMPLE_SLOTS, SAMPLE_GROUP * n_gather, HEAD_DIM, PAGE_SIZE), F32),
                        pltpu.VMEM((SAMPLE_SLOTS, SAMPLE_GROUP * n_gather, HEAD_DIM, PAGE_SIZE), F32),
                        pltpu.SemaphoreType.DMA((2, SAMPLE_SLOTS))],
    )
    return pl.pallas_call(
        functools.partial(_sample_attn_kernel, layer=layer, n_tok=n_tok),
        grid_spec=grid_spec,
        out_shape=jax.ShapeDtypeStruct((n_seq, HEAD_DIM, LANES), F32),
        compiler_params=_params(1),
        name="sample_attn",
    )(top_flat, page_table, q_t, kn_t, vn_t, cache_kt, cache_vt)


def _rope_tables(pos):
    inv = ROPE_THETA ** (-jnp.arange(0, HEAD_DIM, 2, dtype=F32) / HEAD_DIM)
    ang = pos.astype(F32)[:, None] * inv[None, :]
    cos, sin = jnp.cos(ang), jnp.sin(ang)
    cos_g = jnp.concatenate([cos, cos] * HEADS_PER_LANE_GROUP, axis=1)
    sin_g = jnp.concatenate([-sin, sin] * HEADS_PER_LANE_GROUP, axis=1)
    return cos_g, sin_g


def kernel(x_prompt, x_sample, cache_k, cache_v, cache_conv, page_table,
           ffn1_norm, ffn1_w_gate, ffn1_w_up, ffn1_w_down,
           mix_norm, w_in, conv_dw_w, conv_dw_b, conv_ln_g, conv_ln_b,
           w_conv_out, w_attn_out, w_out,
           ffn2_norm, ffn2_w_gate, ffn2_w_up, ffn2_w_down, final_norm):
    bsz, seq, d = x_prompt.shape
    n_seq, n_tok, _ = x_sample.shape
    depth = w_in.shape[0]
    n_pages = page_table.shape[1]
    past = n_pages * PAGE_SIZE
    assert d == N_HEADS * HEAD_DIM and past % MOBA_BLOCK == 0 and n_pages // PAGES_PER_BLOCK >= MOBA_TOP_K
    assert seq % MOBA_BLOCK == 0 and seq >= CONV_WIDTH - 1 and seq // MOBA_BLOCK <= LANES

    tm_ffn, tm_mix, tq = 512, 256, 1024
    n_p, n_s = bsz * seq, n_seq * n_tok

    bf = lambda w: w.astype(BF16)
    row = lambda g: g.reshape(1, -1)

    cos_p, sin_p = _rope_tables(jnp.arange(seq))
    cos_s, sin_s = _rope_tables(past + jnp.repeat(jnp.arange(n_tok), n_seq))

    xp = x_prompt.reshape(n_p, d)
    xs = x_sample.transpose(1, 0, 2).reshape(n_s, d)
    cache_kt = jnp.swapaxes(cache_k, 3, 4)
    cache_vt = jnp.swapaxes(cache_v, 3, 4)
    zero_hist = jnp.zeros((bsz, HALO, d), F32)
    n_comb = n_tok * N_HEADS
    assert n_tok <= SUBLANES

    kp_out, vp_out, convp_out, ks_out, vs_out, convs_out = [], [], [], [], [], []
    for l in range(depth):
        last = l == depth - 1
        f1 = (row(ffn1_norm[l]), bf(ffn1_w_gate[l]), bf(ffn1_w_up[l]), bf(ffn1_w_down[l]))
        f2 = (row(ffn2_norm[l]), bf(ffn2_w_gate[l]), bf(ffn2_w_up[l]), bf(ffn2_w_down[l]))
        fin = row(final_norm) if last else None
        w_in_l = bf(w_in[l])
        mix_w = (conv_dw_w[l], row(conv_dw_b[l]), row(conv_ln_g[l]), row(conv_ln_b[l]),
                 bf(w_conv_out[l]), bf(w_attn_out[l]), bf(w_out[l]))

        xp = _ffn(xp, *f1, None, tm_ffn)
        glu, gc, ga, q, kb, vb, kpg, vpg, kmean = _mix_in(xp, row(mix_norm[l]), w_in_l, cos_p, sin_p, tm_mix, True)
        n_blk = seq // MOBA_BLOCK
        means = jnp.pad(kmean.reshape(bsz, n_blk, d), ((0, 0), (0, LANES - n_blk), (0, 0))).transpose(0, 2, 1)
        attn = _moba_prompt(q.reshape(bsz, seq, d), kb.reshape(bsz, seq, d), vb.reshape(bsz, seq, d), means, tq)
        xp = _mix_out_seq(xp, glu, zero_hist, gc, ga, attn.reshape(n_p, d), *mix_w, tm_mix, seq)
        xp = _ffn(xp, *f2, fin, tm_ffn)
        kp_out.append(kpg.reshape(bsz, seq // PAGE_SIZE, N_HEADS, HEAD_DIM, PAGE_SIZE))
        vp_out.append(vpg.reshape(bsz, seq // PAGE_SIZE, N_HEADS, HEAD_DIM, PAGE_SIZE))
        convp_out.append(glu.reshape(bsz, seq, d)[:, seq - (CONV_WIDTH - 1):])

        xs = _ffn(xs, *f1, None, n_s)
        glu_s, gc_s, ga_s, q_s, k_s, v_s = _mix_in(xs, row(mix_norm[l]), w_in_l, cos_s, sin_s, n_s, False)
        to_seq_major = lambda a: a.reshape(n_tok, n_seq, d).transpose(1, 0, 2)
        k_new = to_seq_major(k_s).reshape(n_seq, n_tok, N_HEADS, HEAD_DIM).transpose(0, 2, 1, 3)
        v_new = to_seq_major(v_s).reshape(n_seq, n_tok, N_HEADS, HEAD_DIM).transpose(0, 2, 1, 3)
        q_b = to_seq_major(q_s)
        top = _sample_gate(page_table, jnp.pad(q_b, ((0, 0), (0, SUBLANES - n_tok), (0, 0))), cache_kt, l)
        top = top.reshape(n_seq, N_HEADS, SUBLANES, LANES)[:, :, :n_tok, :MOBA_TOP_K]
        top_flat = top.transpose(0, 2, 1, 3).reshape(-1)
        lane_pad = lambda a: jnp.pad(a, [(0, 0)] * (a.ndim - 1) + [(0, LANES - a.shape[-1])])
        q_t = lane_pad(q_b.reshape(n_seq, n_comb, HEAD_DIM).transpose(0, 2, 1))
        attn_s = _sample_attn(top_flat, page_table, q_t, lane_pad(k_new.transpose(0, 1, 3, 2)),
                              lane_pad(v_new.transpose(0, 1, 3, 2)), cache_kt, cache_vt, l, n_tok)
        attn_s = attn_s[:, :, :n_comb].transpose(2, 0, 1).reshape(n_tok, N_HEADS, n_seq, HEAD_DIM)
        attn_s = attn_s.transpose(0, 2, 1, 3).reshape(n_s, d)
        ext = jnp.concatenate([cache_conv[l].transpose(1, 0, 2), glu_s.reshape(n_tok, n_seq, d)], axis=0)
        xs = _mix_out_steps(xs, ext, gc_s, ga_s, attn_s, *mix_w)
        xs = _ffn(xs, *f2, fin, n_s)
        ks_out.append(k_new)
        vs_out.append(v_new)
        convs_out.append(jnp.concatenate([cache_conv[l][:, n_tok:], to_seq_major(glu_s)], axis=1))

    y_prompt = xp.reshape(bsz, seq, d)
    y_sample = xs.reshape(n_tok, n_seq, d).transpose(1, 0, 2)
    pages = lambda parts: jnp.swapaxes(jnp.stack(parts), -1, -2)
    return (y_prompt, y_sample, pages(kp_out), pages(vp_out), jnp.stack(convp_out),
            jnp.stack(ks_out), jnp.stack(vs_out), jnp.stack(convs_out))
```
